```python
import math
import jax, jax.numpy as jnp
from jax import lax
import numpy as np

D_MODEL = 2048
BATCH = 4
SEQ = 4096
DEPTH = 1

MIX_WIDTH = D_MODEL
ATTN_WIDTH = MIX_WIDTH // 2
CONV_WIDTH = MIX_WIDTH - ATTN_WIDTH
DIFF_HEAD_DIM = 64
DIFF_V_DIM = 2 * DIFF_HEAD_DIM
N_DIFF_HEADS = ATTN_WIDTH // DIFF_V_DIM
IN_COLS = 3 * ATTN_WIDTH + 2 * CONV_WIDTH
CONV_KERNEL = 31
D_FF = 5632
ROPE_THETA = 10000.0
Q_BLOCK = 128
RMS_EPS = 1e-6
LN_EPS = 1e-5
FFN_RESIDUAL_WEIGHT = 0.5
N_MOD = 9
POS_OFFSET_MAX = 1024

kernel_name = 'hybrid_diffattn_conformer_macaron_block'


def lambda_init(layer_idx):
    return 0.8 - 0.6 * math.exp(-0.3 * layer_idx)


def rms_norm(x, g, eps=RMS_EPS):
    xf = x.astype(jnp.float32)
    y = xf * lax.rsqrt(jnp.mean(xf * xf, axis=-1, keepdims=True) + eps)
    return (y * g.astype(jnp.float32)).astype(x.dtype)


def layer_norm(x, g, b, eps=LN_EPS):
    xf = x.astype(jnp.float32)
    mu = jnp.mean(xf, axis=-1, keepdims=True)
    var = jnp.mean(jnp.square(xf - mu), axis=-1, keepdims=True)
    y = (xf - mu) * lax.rsqrt(var + eps)
    return (y * g.astype(jnp.float32) + b.astype(jnp.float32)).astype(x.dtype)


def modulate(h, shift, scale):
    return h * (1 + scale[:, None, :]) + shift[:, None, :]


def swiglu(h, w_gu, w_down):
    gu = jnp.einsum('bsd,df->bsf', h, w_gu)
    g, u = jnp.split(gu, 2, axis=-1)
    return jnp.einsum('bsf,fd->bsd', jax.nn.silu(g) * u, w_down)


def rope_tables(positions):
    inv_freq = ROPE_THETA ** (-jnp.arange(0, DIFF_HEAD_DIM, 2, dtype=jnp.float32) / DIFF_HEAD_DIM)
    ang = positions.astype(jnp.float32)[..., None] * inv_freq
    return jnp.cos(ang), jnp.sin(ang)


def apply_rope(t, cos, sin):
    cos = cos[:, :, None, None, :]
    sin = sin[:, :, None, None, :]
    tf = t.astype(jnp.float32)
    t1, t2 = jnp.split(tf, 2, axis=-1)
    out = jnp.concatenate([t1 * cos - t2 * sin, t2 * cos + t1 * sin], axis=-1)
    return out.astype(t.dtype)


def diff_attention(q, k, v, lam):
    B, S, H, _, dk = q.shape
    dv = v.shape[-1]
    nb = S // Q_BLOCK
    scale = dk ** -0.5
    qb = q.reshape(B, nb, Q_BLOCK, H, 2, dk).transpose(1, 0, 2, 3, 4, 5)
    starts = jnp.arange(nb, dtype=jnp.int32) * Q_BLOCK
    key_idx = jnp.arange(S, dtype=jnp.int32)

    def block(args):
        qi, start = args
        s = jnp.einsum('bqhmd,bkhmd->bhmqk', qi, k).astype(jnp.float32) * scale
        q_idx = start + jnp.arange(Q_BLOCK, dtype=jnp.int32)
        causal = key_idx[None, :] <= q_idx[:, None]
        p = jax.nn.softmax(jnp.where(causal, s, -jnp.inf), axis=-1)
        a = p[:, :, 0] - lam * p[:, :, 1]
        return jnp.einsum('bhqk,bkhd->bqhd', a.astype(v.dtype), v)

    out = lax.map(block, (qb, starts))
    return out.transpose(1, 0, 2, 3, 4).reshape(B, S, H, dv)


def causal_depthwise_conv(u, w, b):
    K, C = w.shape
    y = lax.conv_general_dilated(u, w[:, None, :].astype(u.dtype), window_strides=(1,),
                                 padding=[(K - 1, 0)], dimension_numbers=('NWC', 'WIO', 'NWC'),
                                 feature_group_count=C)
    return y + b


def hybrid_mixer(h, cos, sin, w_in, q_norm, k_norm, lambda_q1, lambda_k1, lambda_q2, lambda_k2,
                 subln, conv_w, conv_b, conv_ln_g, conv_ln_b, w_out, lam_init):
    B, S, _ = h.shape
    proj = jnp.einsum('bsd,de->bse', h, w_in)
    q, k, v, conv_a, conv_g = jnp.split(
        proj, [ATTN_WIDTH, 2 * ATTN_WIDTH, 3 * ATTN_WIDTH, 3 * ATTN_WIDTH + CONV_WIDTH], axis=-1)
    q = q.reshape(B, S, N_DIFF_HEADS, 2, DIFF_HEAD_DIM)
    k = k.reshape(B, S, N_DIFF_HEADS, 2, DIFF_HEAD_DIM)
    v = v.reshape(B, S, N_DIFF_HEADS, DIFF_V_DIM)
    q = apply_rope(rms_norm(q, q_norm), cos, sin)
    k = apply_rope(rms_norm(k, k_norm), cos, sin)
    lam = (jnp.exp(jnp.sum(lambda_q1.astype(jnp.float32) * lambda_k1.astype(jnp.float32)))
           - jnp.exp(jnp.sum(lambda_q2.astype(jnp.float32) * lambda_k2.astype(jnp.float32)))
           + lam_init)
    attn = diff_attention(q, k, v, lam)
    attn = (rms_norm(attn, subln) * (1.0 - lam_init)).reshape(B, S, ATTN_WIDTH)
    u = conv_a * jax.nn.sigmoid(conv_g)
    u = causal_depthwise_conv(u, conv_w, conv_b)
    u = jax.nn.silu(layer_norm(u, conv_ln_g, conv_ln_b))
    merged = jnp.concatenate([attn, u], axis=-1)
    return jnp.einsum('bse,ed->bsd', merged, w_out)


def setup_inputs(seed: int = 0) -> dict:
    key = jax.random.key(seed)
    ks = jax.random.split(key, 32)
    f32 = jnp.float32
    D, F, L = D_MODEL, D_FF, DEPTH

    def nrm(k, shape, s):
        return jax.random.normal(k, shape, f32) * s

    gate_offset = jnp.repeat(jnp.tile(jnp.array([0.0, 0.0, 1.0], f32), 3), D)
    positions = (jnp.arange(SEQ, dtype=jnp.int32)[None, :]
                 + jax.random.randint(ks[2], (BATCH, 1), 0, POS_OFFSET_MAX, dtype=jnp.int32))
    return {
        'x': nrm(ks[0], (BATCH, SEQ, D), 1.0),
        'c': nrm(ks[1], (BATCH, D), 1.0),
        'positions': positions,
        'w_ada': nrm(ks[3], (L, D, N_MOD * D), 0.1 * D ** -0.5),
        'b_ada': nrm(ks[4], (L, N_MOD * D), 0.02) + gate_offset,
        'ffn1_norm': 1.0 + nrm(ks[5], (L, D), 0.02),
        'ffn1_w_gu': nrm(ks[6], (L, D, 2 * F), D ** -0.5),
        'ffn1_w_down': nrm(ks[7], (L, F, D), F ** -0.5),
        'mix_norm': 1.0 + nrm(ks[8], (L, D), 0.02),
        'w_in': nrm(ks[9], (L, D, IN_COLS), D ** -0.5),
        'q_norm': 1.0 + nrm(ks[10], (L, DIFF_HEAD_DIM), 0.02),
        'k_norm': 1.0 + nrm(ks[11], (L, DIFF_HEAD_DIM), 0.02),
        'lambda_q1': nrm(ks[12], (L, DIFF_HEAD_DIM), 0.1),
        'lambda_k1': nrm(ks[13], (L, DIFF_HEAD_DIM), 0.1),
        'lambda_q2': nrm(ks[14], (L, DIFF_HEAD_DIM), 0.1),
        'lambda_k2': nrm(ks[15], (L, DIFF_HEAD_DIM), 0.1),
        'subln': 1.0 + nrm(ks[16], (L, DIFF_V_DIM), 0.02),
        'conv_w': nrm(ks[17], (L, CONV_KERNEL, CONV_WIDTH), CONV_KERNEL ** -0.5),
        'conv_b': nrm(ks[18], (L, CONV_WIDTH), 0.02),
        'conv_ln_g': 1.0 + nrm(ks[19], (L, CONV_WIDTH), 0.02),
        'conv_ln_b': nrm(ks[20], (L, CONV_WIDTH), 0.02),
        'w_out': nrm(ks[21], (L, MIX_WIDTH, D), MIX_WIDTH ** -0.5),
        'ffn2_norm': 1.0 + nrm(ks[22], (L, D), 0.02),
        'ffn2_w_gu': nrm(ks[23], (L, D, 2 * F), D ** -0.5),
        'ffn2_w_down': nrm(ks[24], (L, F, D), F ** -0.5),
    }


def reference(x, c, positions, w_ada, b_ada, ffn1_norm, ffn1_w_gu, ffn1_w_down, mix_norm, w_in,
              q_norm, k_norm, lambda_q1, lambda_k1, lambda_q2, lambda_k2, subln, conv_w, conv_b,
              conv_ln_g, conv_ln_b, w_out, ffn2_norm, ffn2_w_gu, ffn2_w_down):
    cos, sin = rope_tables(positions)
    c_act = jax.nn.silu(c)
    for l in range(DEPTH):
        ada = jnp.einsum('bd,de->be', c_act, w_ada[l]) + b_ada[l]
        sh1, sc1, g1, shm, scm, gm, sh2, sc2, g2 = jnp.split(ada, N_MOD, axis=-1)
        h = modulate(rms_norm(x, ffn1_norm[l]), sh1, sc1)
        x = x + FFN_RESIDUAL_WEIGHT * g1[:, None, :] * swiglu(h, ffn1_w_gu[l], ffn1_w_down[l])
        h = modulate(rms_norm(x, mix_norm[l]), shm, scm)
        y = hybrid_mixer(h, cos, sin, w_in[l], q_norm[l], k_norm[l], lambda_q1[l], lambda_k1[l],
                         lambda_q2[l], lambda_k2[l], subln[l], conv_w[l], conv_b[l],
                         conv_ln_g[l], conv_ln_b[l], w_out[l], lambda_init(l))
        x = x + gm[:, None, :] * y
        h = modulate(rms_norm(x, ffn2_norm[l]), sh2, sc2)
        x = x + FFN_RESIDUAL_WEIGHT * g2[:, None, :] * swiglu(h, ffn2_w_gu[l], ffn2_w_down[l])
    return x
```

```python
import functools
import math

import jax
import jax.numpy as jnp
from jax import lax
from jax.experimental import pallas as pl
from jax.experimental.pallas import tpu as pltpu

F32 = jnp.float32
BF16 = jnp.bfloat16

D_MODEL = 2048
ATTN_WIDTH = 1024
CONV_WIDTH = 1024
HEAD_DIM = 64
V_DIM = 128
N_HEADS = ATTN_WIDTH // V_DIM
IN_COLS = 3 * ATTN_WIDTH + 2 * CONV_WIDTH
N_IN_SEGMENTS = IN_COLS // 1024
CONV_K = 31
D_FF = 5632
ROPE_THETA = 10000.0
RMS_EPS = 1e-6
LN_EPS = 1e-5
FFN_RES = 0.5
N_MOD = 9

VMEM_LIMIT = 56 * 1024 * 1024
MASK_VALUE = -1e30


def _silu(x):
    return x * jax.nn.sigmoid(x)


def _norm_modulate(x, nw, sh, sc):
    ms = jnp.mean(x * x, axis=-1, keepdims=True)
    y = x * lax.rsqrt(ms + RMS_EPS) * nw
    return y * (1.0 + sc) + sh


def _ada_kernel(c_ref, w_ref, b_ref, o_ref):
    ca = _silu(c_ref[...])
    o_ref[...] = jnp.dot(ca, w_ref[...], preferred_element_type=F32,
                         precision=lax.Precision.HIGHEST) + b_ref[...]


def _ada(c_pad, w, b):
    n = w.shape[1]
    tn = 1024
    return pl.pallas_call(
        _ada_kernel,
        grid=(n // tn,),
        in_specs=[pl.BlockSpec((8, D_MODEL), lambda j: (0, 0)),
                  pl.BlockSpec((D_MODEL, tn), lambda j: (0, j)),
                  pl.BlockSpec((1, tn), lambda j: (0, j))],
        out_specs=pl.BlockSpec((8, tn), lambda j: (0, j)),
        out_shape=jax.ShapeDtypeStruct((8, n), F32),
        compiler_params=pltpu.CompilerParams(dimension_semantics=("arbitrary",),
                                             vmem_limit_bytes=VMEM_LIMIT),
        name="ada",
    )(c_pad, w, b)


def _mod_spec(k, tiles_per_batch):
    return pl.BlockSpec((None, None, 1, D_MODEL),
                        lambda i, j: (i // tiles_per_batch, k, 0, 0))


def _ffn_kernel(x_ref, nw_ref, sh_ref, sc_ref, gate_ref, wg_ref, wu_ref, wd_ref, o_ref, h_ref,
                *, n_f):
    f = pl.program_id(1)

    @pl.when(f == 0)
    def _():
        h = _norm_modulate(x_ref[...], nw_ref[...], sh_ref[...], sc_ref[...])
        h_ref[...] = h.astype(BF16)
        o_ref[...] = jnp.zeros_like(o_ref)

    h = h_ref[...]
    g = jnp.dot(h, wg_ref[...], preferred_element_type=F32)
    u = jnp.dot(h, wu_ref[...], preferred_element_type=F32)
    a = (_silu(g) * u).astype(BF16)
    o_ref[...] += jnp.dot(a, wd_ref[...], preferred_element_type=F32)

    @pl.when(f == n_f - 1)
    def _():
        o_ref[...] = x_ref[...] + (FFN_RES * gate_ref[...]) * o_ref[...]


def _ffn(x2d, nw, ada4, mod_base, w_gu, w_down, seq_len, tm=512, tf=512):
    t = x2d.shape[0]
    n_f = D_FF // tf
    tpb = seq_len // tm
    row = pl.BlockSpec((tm, D_MODEL), lambda i, f: (i, 0))
    return pl.pallas_call(
        functools.partial(_ffn_kernel, n_f=n_f),
        grid=(t // tm, n_f),
        in_specs=[row,
                  pl.BlockSpec((1, D_MODEL), lambda i, f: (0, 0)),
                  _mod_spec(mod_base, tpb), _mod_spec(mod_base + 1, tpb), _mod_spec(mod_base + 2, tpb),
                  pl.BlockSpec((D_MODEL, tf), lambda i, f: (0, f)),
                  pl.BlockSpec((D_MODEL, tf), lambda i, f: (0, f + n_f)),
                  pl.BlockSpec((tf, D_MODEL), lambda i, f: (f, 0))],
        out_specs=row,
        out_shape=jax.ShapeDtypeStruct((t, D_MODEL), F32),
        scratch_shapes=[pltpu.VMEM((tm, D_MODEL), BF16)],
        compiler_params=pltpu.CompilerParams(dimension_semantics=("parallel", "arbitrary"),
                                             vmem_limit_bytes=VMEM_LIMIT),
        name="ffn",
    )(x2d, nw, ada4, ada4, ada4, w_gu, w_gu, w_down)


def _group_mean_sq(y, gmat):
    sq = y * y
    hi = sq.astype(BF16)
    lo = (sq - hi.astype(F32)).astype(BF16)
    cols = []
    for c in range(y.shape[1] // 256):
        sl = slice(c * 256, (c + 1) * 256)
        cols.append(jnp.dot(hi[:, sl], gmat, preferred_element_type=F32)
                    + jnp.dot(lo[:, sl], gmat, preferred_element_type=F32))
    return jnp.concatenate(cols, axis=1)


def _rope(y, cos, sin_signed):
    lane = lax.broadcasted_iota(jnp.int32, (y.shape[0], 128), 1)
    first_half = (lane % HEAD_DIM) < (HEAD_DIM // 2)
    cols = []
    for c in range(y.shape[1] // 128):
        t = y[:, c * 128:(c + 1) * 128]
        rot = jnp.where(first_half, pltpu.roll(t, 96, 1), pltpu.roll(t, 32, 1))
        cols.append(t * cos + rot * sin_signed)
    return jnp.concatenate(cols, axis=1)


def _inproj_kernel(x_ref, nw_ref, sh_ref, sc_ref, w_ref, pos_ref, freq_ref, sign_ref, gmat_ref,
                   qn_ref, kn_ref, q_ref, k_ref, v_ref, u_ref, h_ref, a_ref, cos_ref, sin_ref):
    j = pl.program_id(1)

    @pl.when(j == 0)
    def _():
        h = _norm_modulate(x_ref[...], nw_ref[...], sh_ref[...], sc_ref[...])
        h_ref[...] = h.astype(BF16)
        ang = pos_ref[...].astype(F32) * freq_ref[...]
        cos_ref[...] = jnp.cos(ang)
        sin_ref[...] = jnp.sin(ang) * sign_ref[...]

    res = jnp.dot(h_ref[...], w_ref[...], preferred_element_type=F32)

    def qk_epilogue(norm_w, scale):
        ms = _group_mean_sq(res, gmat_ref[...])
        y = res * lax.rsqrt(ms + RMS_EPS) * norm_w
        return (_rope(y, cos_ref[...], sin_ref[...]) * scale).astype(BF16)

    @pl.when(j == 0)
    def _():
        q_ref[...] = qk_epilogue(qn_ref[...], HEAD_DIM ** -0.5)

    @pl.when(j == 1)
    def _():
        k_ref[...] = qk_epilogue(kn_ref[...], 1.0)

    @pl.when(j == 2)
    def _():
        v_ref[...] = res.astype(BF16)

    @pl.when(j == 3)
    def _():
        a_ref[...] = res

    @pl.when(j == 4)
    def _():
        u_ref[...] = a_ref[...] * jax.nn.sigmoid(res)


def _inproj(x2d, nw, ada4, w_in, pos, freq, sign, gmat, qn, kn, seq_len, tm=512):
    t = x2d.shape[0]
    tpb = seq_len // tm
    const = lambda shape: pl.BlockSpec(shape, lambda i, j: (0, 0))
    out_row = pl.BlockSpec((tm, 1024), lambda i, j: (i, 0))
    return pl.pallas_call(
        _inproj_kernel,
        grid=(t // tm, N_IN_SEGMENTS),
        in_specs=[pl.BlockSpec((tm, D_MODEL), lambda i, j: (i, 0)),
                  const((1, D_MODEL)),
                  _mod_spec(3, tpb), _mod_spec(4, tpb),
                  pl.BlockSpec((D_MODEL, 1024), lambda i, j: (0, j)),
                  pl.BlockSpec((tm, 1), lambda i, j: (i, 0)),
                  const((1, 128)), const((1, 128)), const((256, 256)),
                  const((1, 1024)), const((1, 1024))],
        out_specs=[out_row, out_row, out_row, out_row],
        out_shape=[jax.ShapeDtypeStruct((t, 1024), BF16),
                   jax.ShapeDtypeStruct((t, 1024), BF16),
                   jax.ShapeDtypeStruct((t, 1024), BF16),
                   jax.ShapeDtypeStruct((t, 1024), F32)],
        scratch_shapes=[pltpu.VMEM((tm, D_MODEL), BF16),
                        pltpu.VMEM((tm, 1024), F32),
                        pltpu.VMEM((tm, 128), F32),
                        pltpu.VMEM((tm, 128), F32)],
        compiler_params=pltpu.CompilerParams(dimension_semantics=("parallel", "arbitrary"),
                                             vmem_limit_bytes=VMEM_LIMIT),
        name="inproj",
    )(x2d, nw, ada4, ada4, w_in, pos, freq, sign, gmat, qn, kn)


def _attn_kernel(q_ref, k_ref, v_ref, lq1_ref, lk1_ref, lq2_ref, lk2_ref, subln_ref, o_ref,
                 *, tq, lam_init):
    i = pl.program_id(2)
    lam = (jnp.exp(jnp.sum(lq1_ref[...] * lk1_ref[...], axis=-1, keepdims=True))
           - jnp.exp(jnp.sum(lq2_ref[...] * lk2_ref[...], axis=-1, keepdims=True))
           + lam_init)

    q = q_ref[...]
    lane = lax.broadcasted_iota(jnp.int32, q.shape, 1)
    zero = jnp.zeros_like(q)
    qq = jnp.concatenate([jnp.where(lane < HEAD_DIM, q, zero),
                          jnp.where(lane >= HEAD_DIM, q, zero)], axis=0)

    def step(j, carry, masked):
        m, l, acc = carry
        start = pl.multiple_of(j * tq, tq)
        kb = k_ref[pl.ds(start, tq), :]
        vb = v_ref[pl.ds(start, tq), :]
        s = lax.dot_general(qq, kb, (((1,), (1,)), ((), ())), preferred_element_type=F32)
        if masked:
            row = lax.broadcasted_iota(jnp.int32, s.shape, 0) % tq
            col = lax.broadcasted_iota(jnp.int32, s.shape, 1)
            s = jnp.where(col <= row, s, MASK_VALUE)
        m_new = jnp.maximum(m, jnp.max(s, axis=-1, keepdims=True))
        alpha = jnp.exp(m - m_new)
        p = jnp.exp(s - m_new)
        l = alpha * l + jnp.sum(p, axis=-1, keepdims=True)
        acc = alpha * acc + jnp.dot(p.astype(BF16), vb, preferred_element_type=F32)
        return m_new, l, acc

    init = (jnp.full((2 * tq, 1), MASK_VALUE, F32),
            jnp.zeros((2 * tq, 1), F32),
            jnp.zeros((2 * tq, V_DIM), F32))
    carry = lax.fori_loop(0, i, functools.partial(step, masked=False), init)
    _, l, acc = step(i, carry, True)
    o = acc / l
    d = o[:tq] - lam * o[tq:]
    ms = jnp.mean(d * d, axis=-1, keepdims=True)
    o_ref[...] = (d * lax.rsqrt(ms + RMS_EPS) * subln_ref[...] * (1.0 - lam_init)).astype(BF16)


def _attention(q, k, v, lq1, lk1, lq2, lk2, subln, lam_init, tq=256):
    b, s, _ = q.shape
    qspec = pl.BlockSpec((None, tq, V_DIM), lambda bi, h, i: (bi, i, h))
    kvspec = pl.BlockSpec((None, s, V_DIM), lambda bi, h, i: (bi, 0, h))
    vec = lambda n: pl.BlockSpec((1, n), lambda bi, h, i: (0, 0))
    return pl.pallas_call(
        functools.partial(_attn_kernel, tq=tq, lam_init=lam_init),
        grid=(b, N_HEADS, s // tq),
        in_specs=[qspec, kvspec, kvspec, vec(HEAD_DIM), vec(HEAD_DIM), vec(HEAD_DIM), vec(HEAD_DIM),
                  vec(V_DIM)],
        out_specs=qspec,
        out_shape=jax.ShapeDtypeStruct((b, s, ATTN_WIDTH), BF16),
        compiler_params=pltpu.CompilerParams(
            dimension_semantics=("parallel", "parallel", "arbitrary"),
            vmem_limit_bytes=VMEM_LIMIT),
        name="diffattn",
    )(q, k, v, lq1, lk1, lq2, lk2, subln)


CONV_HALO = 32


def _conv_kernel(u_ref, halo_ref, w_ref, b_ref, g_ref, beta_ref, o_ref, buf_ref, y_ref, *, ts):
    i = pl.program_id(1)
    buf_ref[0:CONV_HALO, :] = jnp.where(i > 0, halo_ref[...], 0.0)
    buf_ref[CONV_HALO:CONV_HALO + ts, :] = u_ref[...]
    rc, cc = 32, 256
    off = CONV_HALO - (CONV_K - 1)
    for r in range(ts // rc):
        for c in range(CONV_WIDTH // cc):
            cs = slice(c * cc, (c + 1) * cc)
            acc = jnp.zeros((rc, cc), F32)
            for t in range(CONV_K):
                lo = r * rc + off + t
                acc = acc + buf_ref[lo:lo + rc, cs] * w_ref[t:t + 1, cs]
            y_ref[r * rc:(r + 1) * rc, cs] = acc + b_ref[:, cs]
    y = y_ref[...]
    mu = jnp.mean(y, axis=-1, keepdims=True)
    yc = y - mu
    var = jnp.mean(yc * yc, axis=-1, keepdims=True)
    z = yc * lax.rsqrt(var + LN_EPS) * g_ref[...] + beta_ref[...]
    o_ref[...] = _silu(z).astype(BF16)


def _conv(u, w, bias, g, beta, ts=256):
    b, s, c = u.shape
    vec = pl.BlockSpec((1, c), lambda bi, i: (0, 0))
    per_halo = ts // CONV_HALO
    return pl.pallas_call(
        functools.partial(_conv_kernel, ts=ts),
        grid=(b, s // ts),
        in_specs=[pl.BlockSpec((None, ts, c), lambda bi, i: (bi, i, 0)),
                  pl.BlockSpec((None, CONV_HALO, c),
                               lambda bi, i: (bi, jnp.maximum(i * per_halo - 1, 0), 0)),
                  pl.BlockSpec((CONV_K, c), lambda bi, i: (0, 0)),
                  vec, vec, vec],
        out_specs=pl.BlockSpec((None, ts, c), lambda bi, i: (bi, i, 0)),
        out_shape=jax.ShapeDtypeStruct((b, s, c), BF16),
        scratch_shapes=[pltpu.VMEM((CONV_HALO + ts, c), F32), pltpu.VMEM((ts, c), F32)],
        compiler_params=pltpu.CompilerParams(dimension_semantics=("parallel", "arbitrary"),
                                             vmem_limit_bytes=VMEM_LIMIT),
        name="conv",
    )(u, u, w, bias, g, beta)


def _outproj_kernel(x_ref, a_ref, c_ref, wa_ref, wc_ref, gate_ref, o_ref):
    y = (jnp.dot(a_ref[...], wa_ref[...], preferred_element_type=F32)
         + jnp.dot(c_ref[...], wc_ref[...], preferred_element_type=F32))
    o_ref[...] = x_ref[...] + gate_ref[...] * y


def _outproj(x2d, attn, conv, w_out, ada4, seq_len, tm=512):
    t = x2d.shape[0]
    tpb = seq_len // tm
    row = pl.BlockSpec((tm, D_MODEL), lambda i: (i, 0))
    half = pl.BlockSpec((tm, 1024), lambda i: (i, 0))
    return pl.pallas_call(
        _outproj_kernel,
        grid=(t // tm,),
        in_specs=[row, half, half,
                  pl.BlockSpec((1024, D_MODEL), lambda i: (0, 0)),
                  pl.BlockSpec((1024, D_MODEL), lambda i: (1, 0)),
                  pl.BlockSpec((None, None, 1, D_MODEL), lambda i: (i // tpb, 5, 0, 0))],
        out_specs=row,
        out_shape=jax.ShapeDtypeStruct((t, D_MODEL), F32),
        compiler_params=pltpu.CompilerParams(dimension_semantics=("parallel",),
                                             vmem_limit_bytes=VMEM_LIMIT),
        name="outproj",
    )(x2d, attn, conv, w_out, w_out, ada4)


def kernel(x, c, positions, w_ada, b_ada, ffn1_norm, ffn1_w_gu, ffn1_w_down, mix_norm, w_in, q_norm, k_norm, lambda_q1, lambda_k1, lambda_q2, lambda_k2, subln, conv_w, conv_b, conv_ln_g, conv_ln_b, w_out, ffn2_norm, ffn2_w_gu, ffn2_w_down):
    bsz, seq, d = x.shape
    depth = w_ada.shape[0]
    t = bsz * seq

    inv_freq = ROPE_THETA ** (-jnp.arange(0, HEAD_DIM, 2, dtype=F32) / HEAD_DIM)
    freq = jnp.tile(inv_freq, 4)[None, :]
    sign = jnp.tile(jnp.concatenate([-jnp.ones(32, F32), jnp.ones(32, F32)]), 2)[None, :]
    grp = jnp.arange(256) // HEAD_DIM
    gmat = jnp.where(grp[:, None] == grp[None, :], 1.0 / HEAD_DIM, 0.0).astype(BF16)
    pos = positions.reshape(t, 1)
    c_pad = jnp.pad(c, ((0, 8 - bsz), (0, 0)))

    x2d = x.reshape(t, d)
    for l in range(depth):
        lam_init = 0.8 - 0.6 * math.exp(-0.3 * l)
        ada = _ada(c_pad, w_ada[l], b_ada[l][None, :])[:bsz]
        ada4 = ada.reshape(bsz, N_MOD, 1, d)

        x2d = _ffn(x2d, ffn1_norm[l][None, :], ada4, 0,
                   ffn1_w_gu[l].astype(BF16), ffn1_w_down[l].astype(BF16), seq)

        q, k, v, u = _inproj(x2d, mix_norm[l][None, :], ada4, w_in[l].astype(BF16), pos, freq, sign,
                             gmat, jnp.tile(q_norm[l], 16)[None, :], jnp.tile(k_norm[l], 16)[None, :],
                             seq)
        attn = _attention(q.reshape(bsz, seq, ATTN_WIDTH), k.reshape(bsz, seq, ATTN_WIDTH),
                          v.reshape(bsz, seq, ATTN_WIDTH),
                          lambda_q1[l][None, :], lambda_k1[l][None, :],
                          lambda_q2[l][None, :], lambda_k2[l][None, :],
                          subln[l][None, :], lam_init)
        cu = _conv(u.reshape(bsz, seq, CONV_WIDTH), conv_w[l], conv_b[l][None, :],
                   conv_ln_g[l][None, :], conv_ln_b[l][None, :])
        x2d = _outproj(x2d, attn.reshape(t, ATTN_WIDTH), cu.reshape(t, CONV_WIDTH),
                       w_out[l].astype(BF16), ada4, seq)

        x2d = _ffn(x2d, ffn2_norm[l][None, :], ada4, 6,
                   ffn2_w_gu[l].astype(BF16), ffn2_w_down[l].astype(BF16), seq)
    return x2d.reshape(bsz, seq, d)
```

```python
import functools
import math

import jax
import jax.numpy as jnp
from jax import lax
from jax.experimental import pallas as pl
from jax.experimental.pallas import tpu as pltpu

F32 = jnp.float32
BF16 = jnp.bfloat16

D_MODEL = 2048
ATTN_WIDTH = 1024
CONV_WIDTH = 1024
HEAD_DIM = 64
V_DIM = 128
N_HEADS = ATTN_WIDTH // V_DIM
IN_COLS = 3 * ATTN_WIDTH + 2 * CONV_WIDTH
N_IN_SEGMENTS = IN_COLS // 1024
CONV_K = 31
D_FF = 5632
ROPE_THETA = 10000.0
RMS_EPS = 1e-6
LN_EPS = 1e-5
FFN_RES = 0.5
N_MOD = 9

VMEM_LIMIT = 56 * 1024 * 1024
MASK_VALUE = -1e30
LOG2_E = math.log2(math.e)


def _silu(x):
    return x * jax.nn.sigmoid(x)


def _norm_modulate(x, nw, sh, sc):
    ms = jnp.mean(x * x, axis=-1, keepdims=True)
    y = x * lax.rsqrt(ms + RMS_EPS) * nw
    return y * (1.0 + sc) + sh


def _ada_kernel(c_ref, w_ref, b_ref, o_ref):
    ca = _silu(c_ref[...])
    o_ref[...] = jnp.dot(ca, w_ref[...], preferred_element_type=F32,
                         precision=lax.Precision.HIGHEST) + b_ref[...]


def _ada(c_pad, w, b):
    n = w.shape[1]
    tn = 1024
    return pl.pallas_call(
        _ada_kernel,
        grid=(n // tn,),
        in_specs=[pl.BlockSpec((8, D_MODEL), lambda j: (0, 0)),
                  pl.BlockSpec((D_MODEL, tn), lambda j: (0, j)),
                  pl.BlockSpec((1, tn), lambda j: (0, j))],
        out_specs=pl.BlockSpec((8, tn), lambda j: (0, j)),
        out_shape=jax.ShapeDtypeStruct((8, n), F32),
        compiler_params=pltpu.CompilerParams(dimension_semantics=("arbitrary",),
                                             vmem_limit_bytes=VMEM_LIMIT),
        name="ada",
    )(c_pad, w, b)


def _mod_spec(k, tiles_per_batch):
    return pl.BlockSpec((None, None, 1, D_MODEL),
                        lambda i, j: (i // tiles_per_batch, k, 0, 0))


def _ffn_kernel(x_ref, nw_ref, sh_ref, sc_ref, gate_ref, wg_ref, wu_ref, wd_ref, o_ref, h_ref,
                *, n_f):
    f = pl.program_id(1)

    @pl.when(f == 0)
    def _():
        h = _norm_modulate(x_ref[...], nw_ref[...], sh_ref[...], sc_ref[...])
        h_ref[...] = h.astype(BF16)
        o_ref[...] = jnp.zeros_like(o_ref)

    h = h_ref[...]
    g = jnp.dot(h, wg_ref[...], preferred_element_type=F32)
    u = jnp.dot(h, wu_ref[...], preferred_element_type=F32)
    a = (_silu(g) * u).astype(BF16)
    o_ref[...] += jnp.dot(a, wd_ref[...], preferred_element_type=F32)

    @pl.when(f == n_f - 1)
    def _():
        o_ref[...] = x_ref[...] + (FFN_RES * gate_ref[...]) * o_ref[...]


def _ffn(x2d, nw, ada4, mod_base, w_gu, w_down, seq_len, tm=512, tf=512):
    t = x2d.shape[0]
    n_f = D_FF // tf
    tpb = seq_len // tm
    row = pl.BlockSpec((tm, D_MODEL), lambda i, f: (i, 0))
    return pl.pallas_call(
        functools.partial(_ffn_kernel, n_f=n_f),
        grid=(t // tm, n_f),
        in_specs=[row,
                  pl.BlockSpec((1, D_MODEL), lambda i, f: (0, 0)),
                  _mod_spec(mod_base, tpb), _mod_spec(mod_base + 1, tpb), _mod_spec(mod_base + 2, tpb),
                  pl.BlockSpec((D_MODEL, tf), lambda i, f: (0, f)),
                  pl.BlockSpec((D_MODEL, tf), lambda i, f: (0, f + n_f)),
                  pl.BlockSpec((tf, D_MODEL), lambda i, f: (f, 0))],
        out_specs=row,
        out_shape=jax.ShapeDtypeStruct((t, D_MODEL), F32),
        scratch_shapes=[pltpu.VMEM((tm, D_MODEL), BF16)],
        compiler_params=pltpu.CompilerParams(dimension_semantics=("parallel", "arbitrary"),
                                             vmem_limit_bytes=VMEM_LIMIT),
        name="ffn",
    )(x2d, nw, ada4, ada4, ada4, w_gu, w_gu, w_down)


def _group_mean_sq(y, gmat):
    sq = y * y
    hi = sq.astype(BF16)
    lo = (sq - hi.astype(F32)).astype(BF16)
    cols = []
    for c in range(y.shape[1] // 256):
        sl = slice(c * 256, (c + 1) * 256)
        cols.append(jnp.dot(hi[:, sl], gmat, preferred_element_type=F32)
                    + jnp.dot(lo[:, sl], gmat, preferred_element_type=F32))
    return jnp.concatenate(cols, axis=1)


def _rope(y, cos, sin_signed):
    lane = lax.broadcasted_iota(jnp.int32, (y.shape[0], 128), 1)
    first_half = (lane % HEAD_DIM) < (HEAD_DIM // 2)
    cols = []
    for c in range(y.shape[1] // 128):
        t = y[:, c * 128:(c + 1) * 128]
        rot = jnp.where(first_half, pltpu.roll(t, 96, 1), pltpu.roll(t, 32, 1))
        cols.append(t * cos + rot * sin_signed)
    return jnp.concatenate(cols, axis=1)


def _inproj_kernel(x_ref, nw_ref, sh_ref, sc_ref, w_ref, pos_ref, freq_ref, sign_ref, gmat_ref,
                   qn_ref, kn_ref, q_ref, k_ref, v_ref, u_ref, h_ref, a_ref, cos_ref, sin_ref):
    j = pl.program_id(1)

    @pl.when(j == 0)
    def _():
        h = _norm_modulate(x_ref[...], nw_ref[...], sh_ref[...], sc_ref[...])
        h_ref[...] = h.astype(BF16)
        ang = pos_ref[...].astype(F32) * freq_ref[...]
        cos_ref[...] = jnp.cos(ang)
        sin_ref[...] = jnp.sin(ang) * sign_ref[...]

    res = jnp.dot(h_ref[...], w_ref[...], preferred_element_type=F32)

    def qk_epilogue(norm_w, scale):
        ms = _group_mean_sq(res, gmat_ref[...])
        y = res * lax.rsqrt(ms + RMS_EPS) * norm_w
        return (_rope(y, cos_ref[...], sin_ref[...]) * scale).astype(BF16)

    @pl.when(j == 0)
    def _():
        q_ref[...] = qk_epilogue(qn_ref[...], HEAD_DIM ** -0.5 * LOG2_E)

    @pl.when(j == 1)
    def _():
        k_ref[...] = qk_epilogue(kn_ref[...], 1.0)

    @pl.when(j == 2)
    def _():
        v_ref[...] = res.astype(BF16)

    @pl.when(j == 3)
    def _():
        a_ref[...] = res

    @pl.when(j == 4)
    def _():
        u_ref[...] = a_ref[...] * jax.nn.sigmoid(res)


def _inproj(x2d, nw, ada4, w_in, pos, freq, sign, gmat, qn, kn, seq_len, tm=512):
    t = x2d.shape[0]
    tpb = seq_len // tm
    const = lambda shape: pl.BlockSpec(shape, lambda i, j: (0, 0))
    out_row = pl.BlockSpec((tm, 1024), lambda i, j: (i, 0))
    return pl.pallas_call(
        _inproj_kernel,
        grid=(t // tm, N_IN_SEGMENTS),
        in_specs=[pl.BlockSpec((tm, D_MODEL), lambda i, j: (i, 0)),
                  const((1, D_MODEL)),
                  _mod_spec(3, tpb), _mod_spec(4, tpb),
                  pl.BlockSpec((D_MODEL, 1024), lambda i, j: (0, j)),
                  pl.BlockSpec((tm, 1), lambda i, j: (i, 0)),
                  const((1, 128)), const((1, 128)), const((256, 256)),
                  const((1, 1024)), const((1, 1024))],
        out_specs=[out_row, out_row, out_row, out_row],
        out_shape=[jax.ShapeDtypeStruct((t, 1024), BF16),
                   jax.ShapeDtypeStruct((t, 1024), BF16),
                   jax.ShapeDtypeStruct((t, 1024), BF16),
                   jax.ShapeDtypeStruct((t, 1024), F32)],
        scratch_shapes=[pltpu.VMEM((tm, D_MODEL), BF16),
                        pltpu.VMEM((tm, 1024), F32),
                        pltpu.VMEM((tm, 128), F32),
                        pltpu.VMEM((tm, 128), F32)],
        compiler_params=pltpu.CompilerParams(dimension_semantics=("parallel", "arbitrary"),
                                             vmem_limit_bytes=VMEM_LIMIT),
        name="inproj",
    )(x2d, nw, ada4, ada4, w_in, pos, freq, sign, gmat, qn, kn)


ATTN_TQ = 512
ATTN_TK = 512


def _attn_kernel(q_ref, k_ref, v_ref, lq1_ref, lk1_ref, lq2_ref, lk2_ref, subln_ref, o_ref,
                 vt_ref, sa_ref, sb_ref, m_ref, l_ref, acc_ref, *, lam_init):
    tq, tk = ATTN_TQ, ATTN_TK
    i = pl.program_id(2)

    @pl.when(i == 0)
    def _():
        for c in range(k_ref.shape[0] // tk):
            vt_ref[c] = v_ref[c * tk:(c + 1) * tk, :].astype(F32).T.astype(BF16)

    lam = (jnp.exp(jnp.sum(lq1_ref[...] * lk1_ref[...], axis=-1, keepdims=True))
           - jnp.exp(jnp.sum(lq2_ref[...] * lk2_ref[...], axis=-1, keepdims=True))
           + lam_init)

    qt = q_ref[...].astype(F32).T
    feat = lax.broadcasted_iota(jnp.int32, qt.shape, 0)
    qqt = jnp.concatenate([jnp.where(feat < HEAD_DIM, qt, 0.0),
                           jnp.where(feat >= HEAD_DIM, qt, 0.0)], axis=1).astype(BF16)

    m_ref[...] = jnp.full(m_ref.shape, MASK_VALUE, F32)
    l_ref[...] = jnp.zeros(l_ref.shape, F32)
    acc_ref[...] = jnp.zeros(acc_ref.shape, F32)

    def scores(c, s_ref):
        start = pl.multiple_of(c * tk, tk)
        s_ref[...] = jnp.dot(k_ref[pl.ds(start, tk), :], qqt, preferred_element_type=F32)

    def softmax_pv(c, s_ref, masked):
        s = s_ref[...]
        if masked:
            key = c * tk + lax.broadcasted_iota(jnp.int32, s.shape, 0)
            qry = i * tq + (lax.broadcasted_iota(jnp.int32, s.shape, 1) & (tq - 1))
            s = jnp.where(key <= qry, s, MASK_VALUE)
        m_old = m_ref[...]
        m_new = jnp.maximum(m_old, jnp.max(s, axis=0, keepdims=True))
        alpha = jnp.exp2(m_old - m_new)
        p = jnp.exp2(s - m_new)
        l_ref[...] = alpha * l_ref[...] + jnp.sum(p, axis=0, keepdims=True)
        acc_ref[...] = alpha * acc_ref[...] + jnp.dot(vt_ref[c], p.astype(BF16),
                                                      preferred_element_type=F32)
        m_ref[...] = m_new

    n = (i * tq) // tk
    scores(0, sa_ref)

    def two_chunks(g, carry):
        scores(2 * g + 1, sb_ref)
        softmax_pv(2 * g, sa_ref, False)
        scores(2 * g + 2, sa_ref)
        softmax_pv(2 * g + 1, sb_ref, False)
        return carry

    lax.fori_loop(0, n // 2, two_chunks, 0)

    @pl.when(n % 2 == 1)
    def _():
        scores(n, sb_ref)
        softmax_pv(n - 1, sa_ref, False)
        softmax_pv(n, sb_ref, True)

    @pl.when(n % 2 == 0)
    def _():
        softmax_pv(n, sa_ref, True)

    o = acc_ref[...] / l_ref[...]
    d = o[:, :tq] - lam * o[:, tq:]
    ms = jnp.mean(d * d, axis=0, keepdims=True)
    y = (d * lax.rsqrt(ms + RMS_EPS)).T
    o_ref[...] = (y * subln_ref[...] * (1.0 - lam_init)).astype(BF16)


def _attention(q, k, v, lq1, lk1, lq2, lk2, subln, lam_init):
    b, s, _ = q.shape
    tq = ATTN_TQ
    qspec = pl.BlockSpec((None, tq, V_DIM), lambda bi, h, i: (bi, i, h))
    kvspec = pl.BlockSpec((None, s, V_DIM), lambda bi, h, i: (bi, 0, h))
    vec = lambda n: pl.BlockSpec((1, n), lambda bi, h, i: (0, 0))
    return pl.pallas_call(
        functools.partial(_attn_kernel, lam_init=lam_init),
        grid=(b, N_HEADS, s // tq),
        in_specs=[qspec, kvspec, kvspec, vec(HEAD_DIM), vec(HEAD_DIM), vec(HEAD_DIM), vec(HEAD_DIM),
                  vec(V_DIM)],
        out_specs=qspec,
        out_shape=jax.ShapeDtypeStruct((b, s, ATTN_WIDTH), BF16),
        scratch_shapes=[pltpu.VMEM((s // ATTN_TK, V_DIM, ATTN_TK), BF16),
                        pltpu.VMEM((ATTN_TK, 2 * tq), F32),
                        pltpu.VMEM((ATTN_TK, 2 * tq), F32),
                        pltpu.VMEM((1, 2 * tq), F32),
                        pltpu.VMEM((1, 2 * tq), F32),
                        pltpu.VMEM((V_DIM, 2 * tq), F32)],
        compiler_params=pltpu.CompilerParams(
            dimension_semantics=("parallel", "parallel", "arbitrary"),
            vmem_limit_bytes=VMEM_LIMIT),
        name="diffattn",
    )(q, k, v, lq1, lk1, lq2, lk2, subln)


CONV_HALO = 32
CONV_TS = 256


def _conv_kernel(u_ref, halo_ref, w_ref, b_ref, g_ref, beta_ref, o_ref, buf_ref, sh_ref, w8_ref,
                 y_ref):
    ts = CONV_TS
    i = pl.program_id(1)
    buf_ref[0:CONV_HALO, :] = jnp.where(i > 0, halo_ref[...], 0.0)
    buf_ref[CONV_HALO:CONV_HALO + ts, :] = u_ref[...]
    n_sh, piece = sh_ref.shape[1], 40
    for s in range(1, 8):
        for r0 in range(0, n_sh, piece):
            sh_ref[s - 1, r0:r0 + piece, :] = buf_ref[s + r0:s + r0 + piece, :]
    for t in range(CONV_K):
        w8_ref[t] = jnp.broadcast_to(w_ref[t:t + 1, :], (8, CONV_WIDTH))
    rc, cc = 32, 256
    first = CONV_HALO - (CONV_K - 1)

    def row_block(r, carry):
        base = pl.multiple_of(r * rc, rc)
        for c in range(CONV_WIDTH // cc):
            cs = slice(c * cc, (c + 1) * cc)
            accs = [None] * (rc // 8)
            for t in range(CONV_K):
                off = first + t
                s = off % 8
                src = buf_ref if s == 0 else sh_ref.at[s - 1]
                wv = w8_ref[t, :, cs]
                for q in range(rc // 8):
                    term = src[pl.ds(base + (off - s + 8 * q), 8), cs] * wv
                    accs[q] = term if accs[q] is None else accs[q] + term
            for q in range(rc // 8):
                y_ref[pl.ds(base + 8 * q, 8), cs] = accs[q] + b_ref[:, cs]
        return carry

    lax.fori_loop(0, ts // rc, row_block, 0)
    y = y_ref[...]
    mu = jnp.mean(y, axis=-1, keepdims=True)
    yc = y - mu
    var = jnp.mean(yc * yc, axis=-1, keepdims=True)
    z = yc * lax.rsqrt(var + LN_EPS) * g_ref[...] + beta_ref[...]
    o_ref[...] = _silu(z).astype(BF16)


def _conv(u, w, bias, g, beta):
    b, s, c = u.shape
    ts = CONV_TS
    vec = pl.BlockSpec((1, c), lambda bi, i: (0, 0))
    per_halo = ts // CONV_HALO
    return pl.pallas_call(
        _conv_kernel,
        grid=(b, s // ts),
        in_specs=[pl.BlockSpec((None, ts, c), lambda bi, i: (bi, i, 0)),
                  pl.BlockSpec((None, CONV_HALO, c),
                               lambda bi, i: (bi, jnp.maximum(i * per_halo - 1, 0), 0)),
                  pl.BlockSpec((CONV_K, c), lambda bi, i: (0, 0)),
                  vec, vec, vec],
        out_specs=pl.BlockSpec((None, ts, c), lambda bi, i: (bi, i, 0)),
        out_shape=jax.ShapeDtypeStruct((b, s, c), BF16),
        scratch_shapes=[pltpu.VMEM((CONV_HALO + ts, c), F32),
                        pltpu.VMEM((7, CONV_HALO + ts - 8, c), F32),
                        pltpu.VMEM((CONV_K, 8, c), F32),
                        pltpu.VMEM((ts, c), F32)],
        compiler_params=pltpu.CompilerParams(dimension_semantics=("parallel", "arbitrary"),
                                             vmem_limit_bytes=VMEM_LIMIT),
        name="conv",
    )(u, u, w, bias, g, beta)


def _outproj_kernel(x_ref, a_ref, c_ref, wa_ref, wc_ref, gate_ref, o_ref):
    y = (jnp.dot(a_ref[...], wa_ref[...], preferred_element_type=F32)
         + jnp.dot(c_ref[...], wc_ref[...], preferred_element_type=F32))
    o_ref[...] = x_ref[...] + gate_ref[...] * y


def _outproj(x2d, attn, conv, w_out, ada4, seq_len, tm=512):
    t = x2d.shape[0]
    tpb = seq_len // tm
    row = pl.BlockSpec((tm, D_MODEL), lambda i: (i, 0))
    half = pl.BlockSpec((tm, 1024), lambda i: (i, 0))
    return pl.pallas_call(
        _outproj_kernel,
        grid=(t // tm,),
        in_specs=[row, half, half,
                  pl.BlockSpec((1024, D_MODEL), lambda i: (0, 0)),
                  pl.BlockSpec((1024, D_MODEL), lambda i: (1, 0)),
                  pl.BlockSpec((None, None, 1, D_MODEL), lambda i: (i // tpb, 5, 0, 0))],
        out_specs=row,
        out_shape=jax.ShapeDtypeStruct((t, D_MODEL), F32),
        compiler_params=pltpu.CompilerParams(dimension_semantics=("parallel",),
                                             vmem_limit_bytes=VMEM_LIMIT),
        name="outproj",
    )(x2d, attn, conv, w_out, w_out, ada4)


def kernel(x, c, positions, w_ada, b_ada, ffn1_norm, ffn1_w_gu, ffn1_w_down, mix_norm, w_in, q_norm, k_norm, lambda_q1, lambda_k1, lambda_q2, lambda_k2, subln, conv_w, conv_b, conv_ln_g, conv_ln_b, w_out, ffn2_norm, ffn2_w_gu, ffn2_w_down):
    bsz, seq, d = x.shape
    depth = w_ada.shape[0]
    t = bsz * seq

    inv_freq = ROPE_THETA ** (-jnp.arange(0, HEAD_DIM, 2, dtype=F32) / HEAD_DIM)
    freq = jnp.tile(inv_freq, 4)[None, :]
    sign = jnp.tile(jnp.concatenate([-jnp.ones(32, F32), jnp.ones(32, F32)]), 2)[None, :]
    grp = jnp.arange(256) // HEAD_DIM
    gmat = jnp.where(grp[:, None] == grp[None, :], 1.0 / HEAD_DIM, 0.0).astype(BF16)
    pos = positions.reshape(t, 1)
    c_pad = jnp.pad(c, ((0, 8 - bsz), (0, 0)))

    x2d = x.reshape(t, d)
    for l in range(depth):
        lam_init = 0.8 - 0.6 * math.exp(-0.3 * l)
        ada = _ada(c_pad, w_ada[l], b_ada[l][None, :])[:bsz]
        ada4 = ada.reshape(bsz, N_MOD, 1, d)

        x2d = _ffn(x2d, ffn1_norm[l][None, :], ada4, 0,
                   ffn1_w_gu[l].astype(BF16), ffn1_w_down[l].astype(BF16), seq)

        q, k, v, u = _inproj(x2d, mix_norm[l][None, :], ada4, w_in[l].astype(BF16), pos, freq, sign,
                             gmat, jnp.tile(q_norm[l], 16)[None, :], jnp.tile(k_norm[l], 16)[None, :],
                             seq)
        attn = _attention(q.reshape(bsz, seq, ATTN_WIDTH), k.reshape(bsz, seq, ATTN_WIDTH),
                          v.reshape(bsz, seq, ATTN_WIDTH),
                          lambda_q1[l][None, :], lambda_k1[l][None, :],
                          lambda_q2[l][None, :], lambda_k2[l][None, :],
                          subln[l][None, :], lam_init)
        cu = _conv(u.reshape(bsz, seq, CONV_WIDTH), conv_w[l], conv_b[l][None, :],
                   conv_ln_g[l][None, :], conv_ln_b[l][None, :])
        x2d = _outproj(x2d, attn.reshape(t, ATTN_WIDTH), cu.reshape(t, CONV_WIDTH),
                       w_out[l].astype(BF16), ada4, seq)

        x2d = _ffn(x2d, ffn2_norm[l][None, :], ada4, 6,
                   ffn2_w_gu[l].astype(BF16), ffn2_w_down[l].astype(BF16), seq)
    return x2d.reshape(bsz, seq, d)
```

```python
import functools
import math

import jax
import jax.numpy as jnp
from jax import lax
from jax.experimental import pallas as pl
from jax.experimental.pallas import tpu as pltpu

F32 = jnp.float32
BF16 = jnp.bfloat16

D_MODEL = 2048
ATTN_WIDTH = 1024
CONV_WIDTH = 1024
HEAD_DIM = 64
V_DIM = 128
N_HEADS = ATTN_WIDTH // V_DIM
IN_COLS = 3 * ATTN_WIDTH + 2 * CONV_WIDTH
N_IN_SEGMENTS = IN_COLS // 1024
CONV_K = 31
D_FF = 5632
ROPE_THETA = 10000.0
RMS_EPS = 1e-6
LN_EPS = 1e-5
FFN_RES = 0.5
N_MOD = 9

VMEM_LIMIT = 56 * 1024 * 1024
MASK_VALUE = -1e30
LOG2_E = math.log2(math.e)


def _silu(x):
    return x * jax.nn.sigmoid(x)


NORM_ROWS = 16


def _norm_modulate(h_ref, x_ref, nw_ref, sh_ref, sc_ref):
    nw, sh, sc1 = nw_ref[...], sh_ref[...], 1.0 + sc_ref[...]
    for r in range(x_ref.shape[0] // NORM_ROWS):
        rows = slice(r * NORM_ROWS, (r + 1) * NORM_ROWS)
        x = x_ref[rows, :]
        ms = jnp.mean(x * x, axis=-1, keepdims=True)
        y = x * lax.rsqrt(ms + RMS_EPS) * nw
        h_ref[rows, :] = (y * sc1 + sh).astype(BF16)


def _ada_kernel(c_ref, w_ref, b_ref, o_ref):
    ca = _silu(c_ref[...])
    o_ref[...] = jnp.dot(ca, w_ref[...], preferred_element_type=F32,
                         precision=lax.Precision.HIGHEST) + b_ref[...]


def _ada(c_pad, w, b):
    n = w.shape[1]
    tn = 1024
    return pl.pallas_call(
        _ada_kernel,
        grid=(n // tn,),
        in_specs=[pl.BlockSpec((8, D_MODEL), lambda j: (0, 0)),
                  pl.BlockSpec((D_MODEL, tn), lambda j: (0, j)),
                  pl.BlockSpec((1, tn), lambda j: (0, j))],
        out_specs=pl.BlockSpec((8, tn), lambda j: (0, j)),
        out_shape=jax.ShapeDtypeStruct((8, n), F32),
        compiler_params=pltpu.CompilerParams(dimension_semantics=("arbitrary",),
                                             vmem_limit_bytes=VMEM_LIMIT),
        name="ada",
    )(c_pad, w, b)


def _mod_spec(k, tiles_per_batch):
    return pl.BlockSpec((None, None, 1, D_MODEL),
                        lambda i, *_: (i // tiles_per_batch, k, 0, 0))


def _ffn_kernel(x_ref, nw_ref, sh_ref, sc_ref, gate_ref, wg_ref, wu_ref, wd_ref, o_ref, h_ref,
                *, n_f):
    f = pl.program_id(1)

    @pl.when(f == 0)
    def _():
        _norm_modulate(h_ref, x_ref, nw_ref, sh_ref, sc_ref)
        o_ref[...] = jnp.zeros_like(o_ref)

    h = h_ref[...]
    g = jnp.dot(h, wg_ref[...], preferred_element_type=F32)
    u = jnp.dot(h, wu_ref[...], preferred_element_type=F32)
    a = (_silu(g) * u).astype(BF16)
    o_ref[...] += jnp.dot(a, wd_ref[...], preferred_element_type=F32)

    @pl.when(f == n_f - 1)
    def _():
        o_ref[...] = x_ref[...] + (FFN_RES * gate_ref[...]) * o_ref[...]


def _ffn(x2d, nw, ada4, mod_base, w_gu, w_down, seq_len, tm=1024, tf=512):
    t = x2d.shape[0]
    n_f = D_FF // tf
    tpb = seq_len // tm
    row = pl.BlockSpec((tm, D_MODEL), lambda i, f: (i, 0))
    return pl.pallas_call(
        functools.partial(_ffn_kernel, n_f=n_f),
        grid=(t // tm, n_f),
        in_specs=[row,
                  pl.BlockSpec((1, D_MODEL), lambda i, f: (0, 0)),
                  _mod_spec(mod_base, tpb), _mod_spec(mod_base + 1, tpb), _mod_spec(mod_base + 2, tpb),
                  pl.BlockSpec((D_MODEL, tf), lambda i, f: (0, f)),
                  pl.BlockSpec((D_MODEL, tf), lambda i, f: (0, f + n_f)),
                  pl.BlockSpec((tf, D_MODEL), lambda i, f: (f, 0))],
        out_specs=row,
        out_shape=jax.ShapeDtypeStruct((t, D_MODEL), F32),
        scratch_shapes=[pltpu.VMEM((tm, D_MODEL), BF16)],
        compiler_params=pltpu.CompilerParams(dimension_semantics=("parallel", "arbitrary"),
                                             vmem_limit_bytes=VMEM_LIMIT),
        name="ffn",
    )(x2d, nw, ada4, ada4, ada4, w_gu, w_gu, w_down)


def _group_mean_sq(y, gmat):
    sq = y * y
    hi = sq.astype(BF16)
    lo = (sq - hi.astype(F32)).astype(BF16)
    cols = []
    for c in range(y.shape[1] // 256):
        sl = slice(c * 256, (c + 1) * 256)
        cols.append(jnp.dot(hi[:, sl], gmat, preferred_element_type=F32)
                    + jnp.dot(lo[:, sl], gmat, preferred_element_type=F32))
    return jnp.concatenate(cols, axis=1)


def _rope(y, cos, sin_signed):
    lane = lax.broadcasted_iota(jnp.int32, (y.shape[0], 128), 1)
    first_half = (lane % HEAD_DIM) < (HEAD_DIM // 2)
    cols = []
    for c in range(y.shape[1] // 128):
        t = y[:, c * 128:(c + 1) * 128]
        rot = jnp.where(first_half, pltpu.roll(t, 96, 1), pltpu.roll(t, 32, 1))
        cols.append(t * cos + rot * sin_signed)
    return jnp.concatenate(cols, axis=1)


INPROJ_ROWS = 256


def _inproj_kernel(x_ref, nw_ref, sh_ref, sc_ref, w_ref, pos_ref, freq_ref, sign_ref, gmat_ref,
                   qn_ref, kn_ref, q_ref, k_ref, v_ref, u_ref, h_ref, cos_ref, sin_ref):
    _norm_modulate(h_ref, x_ref, nw_ref, sh_ref, sc_ref)
    ang = pos_ref[...].astype(F32) * freq_ref[...]
    cos_ref[...] = jnp.cos(ang)
    sin_ref[...] = jnp.sin(ang) * sign_ref[...]

    def segment(rows, j):
        return jnp.dot(h_ref[rows, :], w_ref[:, j * 1024:(j + 1) * 1024],
                       preferred_element_type=F32)

    def qk_epilogue(res, rows, norm_w, scale):
        ms = _group_mean_sq(res, gmat_ref[...])
        y = res * lax.rsqrt(ms + RMS_EPS) * norm_w
        return (_rope(y, cos_ref[rows, :], sin_ref[rows, :]) * scale).astype(BF16)

    def store_heads(o_ref, rows, val):
        for hd in range(N_HEADS):
            o_ref[hd, rows, :] = val[:, hd * V_DIM:(hd + 1) * V_DIM]

    for r in range(x_ref.shape[0] // INPROJ_ROWS):
        rows = slice(r * INPROJ_ROWS, (r + 1) * INPROJ_ROWS)
        store_heads(q_ref, rows,
                    qk_epilogue(segment(rows, 0), rows, qn_ref[...], HEAD_DIM ** -0.5 * LOG2_E))
        store_heads(k_ref, rows, qk_epilogue(segment(rows, 1), rows, kn_ref[...], 1.0))
        store_heads(v_ref, rows, segment(rows, 2).astype(BF16))
        u_ref[rows, :] = segment(rows, 3) * jax.nn.sigmoid(segment(rows, 4))


def _inproj(x2d, nw, ada4, w_in, pos, freq, sign, gmat, qn, kn, seq_len, tm=512):
    t = x2d.shape[0]
    tpb = seq_len // tm
    const = lambda shape: pl.BlockSpec(shape, lambda i: (0, 0))
    out_row = pl.BlockSpec((tm, 1024), lambda i: (i, 0))
    head_major = pl.BlockSpec((None, N_HEADS, tm, V_DIM), lambda i: (i // tpb, 0, i % tpb, 0))
    return pl.pallas_call(
        _inproj_kernel,
        grid=(t // tm,),
        in_specs=[pl.BlockSpec((tm, D_MODEL), lambda i: (i, 0)),
                  const((1, D_MODEL)),
                  _mod_spec(3, tpb), _mod_spec(4, tpb),
                  pl.BlockSpec((D_MODEL, IN_COLS), lambda i: (0, 0), pipeline_mode=pl.Buffered(1)),
                  pl.BlockSpec((tm, 1), lambda i: (i, 0)),
                  const((1, 128)), const((1, 128)), const((256, 256)),
                  const((1, 1024)), const((1, 1024))],
        out_specs=[head_major, head_major, head_major, out_row],
        out_shape=[jax.ShapeDtypeStruct((t // seq_len, N_HEADS, seq_len, V_DIM), BF16),
                   jax.ShapeDtypeStruct((t // seq_len, N_HEADS, seq_len, V_DIM), BF16),
                   jax.ShapeDtypeStruct((t // seq_len, N_HEADS, seq_len, V_DIM), BF16),
                   jax.ShapeDtypeStruct((t, 1024), F32)],
        scratch_shapes=[pltpu.VMEM((tm, D_MODEL), BF16),
                        pltpu.VMEM((tm, 128), F32),
                        pltpu.VMEM((tm, 128), F32)],
        compiler_params=pltpu.CompilerParams(dimension_semantics=("parallel",),
                                             vmem_limit_bytes=VMEM_LIMIT),
        name="inproj",
    )(x2d, nw, ada4, ada4, w_in, pos, freq, sign, gmat, qn, kn)


ATTN_TQ = 512
ATTN_TK = 512


def _attn_kernel(q_ref, k_ref, v_ref, lq1_ref, lk1_ref, lq2_ref, lk2_ref, subln_ref, o_ref,
                 vt_ref, sa_ref, sb_ref, m_ref, l_ref, acc_ref, *, lam_init):
    tq, tk = ATTN_TQ, ATTN_TK
    i = pl.program_id(2)

    @pl.when(i == 0)
    def _():
        for c in range(k_ref.shape[0] // tk):
            vt_ref[c] = v_ref[c * tk:(c + 1) * tk, :].astype(F32).T.astype(BF16)

    lam = (jnp.exp(jnp.sum(lq1_ref[...] * lk1_ref[...], axis=-1, keepdims=True))
           - jnp.exp(jnp.sum(lq2_ref[...] * lk2_ref[...], axis=-1, keepdims=True))
           + lam_init)

    qt = q_ref[...].astype(F32).T
    feat = lax.broadcasted_iota(jnp.int32, qt.shape, 0)
    qqt = jnp.concatenate([jnp.where(feat < HEAD_DIM, qt, 0.0),
                           jnp.where(feat >= HEAD_DIM, qt, 0.0)], axis=1).astype(BF16)

    m_ref[...] = jnp.full(m_ref.shape, MASK_VALUE, F32)
    l_ref[...] = jnp.zeros(l_ref.shape, F32)
    acc_ref[...] = jnp.zeros(acc_ref.shape, F32)

    def scores(c, s_ref):
        start = pl.multiple_of(c * tk, tk)
        s_ref[...] = jnp.dot(k_ref[pl.ds(start, tk), :], qqt, preferred_element_type=F32)

    def softmax_pv(c, s_ref, masked):
        s = s_ref[...]
        if masked:
            key = c * tk + lax.broadcasted_iota(jnp.int32, s.shape, 0)
            qry = i * tq + (lax.broadcasted_iota(jnp.int32, s.shape, 1) & (tq - 1))
            s = jnp.where(key <= qry, s, MASK_VALUE)
        m_old = m_ref[...]
        m_new = jnp.maximum(m_old, jnp.max(s, axis=0, keepdims=True))
        alpha = jnp.exp2(m_old - m_new)
        p = jnp.exp2(s - m_new)
        l_ref[...] = alpha * l_ref[...] + jnp.sum(p, axis=0, keepdims=True)
        acc_ref[...] = alpha * acc_ref[...] + jnp.dot(vt_ref[c], p.astype(BF16),
                                                      preferred_element_type=F32)
        m_ref[...] = m_new

    n = (i * tq) // tk
    scores(0, sa_ref)

    def two_chunks(g, carry):
        scores(2 * g + 1, sb_ref)
        softmax_pv(2 * g, sa_ref, False)
        scores(2 * g + 2, sa_ref)
        softmax_pv(2 * g + 1, sb_ref, False)
        return carry

    lax.fori_loop(0, n // 2, two_chunks, 0)

    @pl.when(n % 2 == 1)
    def _():
        scores(n, sb_ref)
        softmax_pv(n - 1, sa_ref, False)
        softmax_pv(n, sb_ref, True)

    @pl.when(n % 2 == 0)
    def _():
        softmax_pv(n, sa_ref, True)

    o = acc_ref[...] / l_ref[...]
    d = o[:, :tq] - lam * o[:, tq:]
    ms = jnp.mean(d * d, axis=0, keepdims=True)
    y = (d * lax.rsqrt(ms + RMS_EPS)).T
    o_ref[...] = (y * subln_ref[...] * (1.0 - lam_init)).astype(BF16)


def _attention(q, k, v, lq1, lk1, lq2, lk2, subln, lam_init):
    b, _, s, _ = q.shape
    tq = ATTN_TQ
    qspec = pl.BlockSpec((None, None, tq, V_DIM), lambda bi, h, i: (bi, h, i, 0))
    kvspec = pl.BlockSpec((None, None, s, V_DIM), lambda bi, h, i: (bi, h, 0, 0))
    ospec = pl.BlockSpec((None, tq, V_DIM), lambda bi, h, i: (bi, i, h))
    vec = lambda n: pl.BlockSpec((1, n), lambda bi, h, i: (0, 0))
    return pl.pallas_call(
        functools.partial(_attn_kernel, lam_init=lam_init),
        grid=(b, N_HEADS, s // tq),
        in_specs=[qspec, kvspec, kvspec, vec(HEAD_DIM), vec(HEAD_DIM), vec(HEAD_DIM), vec(HEAD_DIM),
                  vec(V_DIM)],
        out_specs=ospec,
        out_shape=jax.ShapeDtypeStruct((b, s, ATTN_WIDTH), BF16),
        scratch_shapes=[pltpu.VMEM((s // ATTN_TK, V_DIM, ATTN_TK), BF16),
                        pltpu.VMEM((ATTN_TK, 2 * tq), F32),
                        pltpu.VMEM((ATTN_TK, 2 * tq), F32),
                        pltpu.VMEM((1, 2 * tq), F32),
                        pltpu.VMEM((1, 2 * tq), F32),
                        pltpu.VMEM((V_DIM, 2 * tq), F32)],
        compiler_params=pltpu.CompilerParams(
            dimension_semantics=("parallel", "parallel", "arbitrary"),
            vmem_limit_bytes=VMEM_LIMIT),
        name="diffattn",
    )(q, k, v, lq1, lk1, lq2, lk2, subln)


CONV_HALO = 32
CONV_TS = 256


def _conv_kernel(u_ref, halo_ref, w_ref, b_ref, g_ref, beta_ref, o_ref, buf_ref, sh_ref, w8_ref,
                 y_ref):
    ts = CONV_TS
    i = pl.program_id(1)
    buf_ref[0:CONV_HALO, :] = jnp.where(i > 0, halo_ref[...], 0.0)
    buf_ref[CONV_HALO:CONV_HALO + ts, :] = u_ref[...]
    n_sh, piece = sh_ref.shape[1], 40
    for s in range(1, 8):
        for r0 in range(0, n_sh, piece):
            sh_ref[s - 1, r0:r0 + piece, :] = buf_ref[s + r0:s + r0 + piece, :]
    for t in range(CONV_K):
        w8_ref[t] = jnp.broadcast_to(w_ref[t:t + 1, :], (8, CONV_WIDTH))
    rc, cc = 32, 256
    first = CONV_HALO - (CONV_K - 1)

    def taps(base):
        for c in range(CONV_WIDTH // cc):
            cs = slice(c * cc, (c + 1) * cc)
            accs = [None] * (rc // 8)
            for t in range(CONV_K):
                off = first + t
                s = off % 8
                src = buf_ref if s == 0 else sh_ref.at[s - 1]
                wv = w8_ref[t, :, cs]
                for q in range(rc // 8):
                    term = src[pl.ds(base + (off - s + 8 * q), 8), cs] * wv
                    accs[q] = term if accs[q] is None else accs[q] + term
            for q in range(rc // 8):
                y_ref[pl.ds(base + 8 * q, 8), cs] = accs[q] + b_ref[:, cs]

    def row_block(r, carry):
        taps(pl.multiple_of(r * rc, rc))
        return carry

    lax.fori_loop(0, ts // rc, row_block, 0)
    y = y_ref[...]
    mu = jnp.mean(y, axis=-1, keepdims=True)
    yc = y - mu
    var = jnp.mean(yc * yc, axis=-1, keepdims=True)
    z = yc * lax.rsqrt(var + LN_EPS) * g_ref[...] + beta_ref[...]
    o_ref[...] = _silu(z).astype(BF16)


def _conv(u, w, bias, g, beta):
    b, s, c = u.shape
    ts = CONV_TS
    vec = pl.BlockSpec((1, c), lambda bi, i: (0, 0))
    per_halo = ts // CONV_HALO
    return pl.pallas_call(
        _conv_kernel,
        grid=(b, s // ts),
        in_specs=[pl.BlockSpec((None, ts, c), lambda bi, i: (bi, i, 0)),
                  pl.BlockSpec((None, CONV_HALO, c),
                               lambda bi, i: (bi, jnp.maximum(i * per_halo - 1, 0), 0)),
                  pl.BlockSpec((CONV_K, c), lambda bi, i: (0, 0)),
                  vec, vec, vec],
        out_specs=pl.BlockSpec((None, ts, c), lambda bi, i: (bi, i, 0)),
        out_shape=jax.ShapeDtypeStruct((b, s, c), BF16),
        scratch_shapes=[pltpu.VMEM((CONV_HALO + ts, c), F32),
                        pltpu.VMEM((7, CONV_HALO + ts - 8, c), F32),
                        pltpu.VMEM((CONV_K, 8, c), F32),
                        pltpu.VMEM((ts, c), F32)],
        compiler_params=pltpu.CompilerParams(dimension_semantics=("parallel", "arbitrary"),
                                             vmem_limit_bytes=VMEM_LIMIT),
        name="conv",
    )(u, u, w, bias, g, beta)


def _outproj_kernel(x_ref, a_ref, c_ref, wa_ref, wc_ref, gate_ref, o_ref):
    y = (jnp.dot(a_ref[...], wa_ref[...], preferred_element_type=F32)
         + jnp.dot(c_ref[...], wc_ref[...], preferred_element_type=F32))
    o_ref[...] = x_ref[...] + gate_ref[...] * y


def _outproj(x2d, attn, conv, w_out, ada4, seq_len, tm=512):
    t = x2d.shape[0]
    tpb = seq_len // tm
    row = pl.BlockSpec((tm, D_MODEL), lambda i: (i, 0))
    half = pl.BlockSpec((tm, 1024), lambda i: (i, 0))
    return pl.pallas_call(
        _outproj_kernel,
        grid=(t // tm,),
        in_specs=[row, half, half,
                  pl.BlockSpec((1024, D_MODEL), lambda i: (0, 0)),
                  pl.BlockSpec((1024, D_MODEL), lambda i: (1, 0)),
                  pl.BlockSpec((None, None, 1, D_MODEL), lambda i: (i // tpb, 5, 0, 0))],
        out_specs=row,
        out_shape=jax.ShapeDtypeStruct((t, D_MODEL), F32),
        compiler_params=pltpu.CompilerParams(dimension_semantics=("parallel",),
                                             vmem_limit_bytes=VMEM_LIMIT),
        name="outproj",
    )(x2d, attn, conv, w_out, w_out, ada4)


def kernel(x, c, positions, w_ada, b_ada, ffn1_norm, ffn1_w_gu, ffn1_w_down, mix_norm, w_in, q_norm, k_norm, lambda_q1, lambda_k1, lambda_q2, lambda_k2, subln, conv_w, conv_b, conv_ln_g, conv_ln_b, w_out, ffn2_norm, ffn2_w_gu, ffn2_w_down):
    bsz, seq, d = x.shape
    depth = w_ada.shape[0]
    t = bsz * seq

    inv_freq = ROPE_THETA ** (-jnp.arange(0, HEAD_DIM, 2, dtype=F32) / HEAD_DIM)
    freq = jnp.tile(inv_freq, 4)[None, :]
    sign = jnp.tile(jnp.concatenate([-jnp.ones(32, F32), jnp.ones(32, F32)]), 2)[None, :]
    grp = jnp.arange(256) // HEAD_DIM
    gmat = jnp.where(grp[:, None] == grp[None, :], 1.0 / HEAD_DIM, 0.0).astype(BF16)
    pos = positions.reshape(t, 1)
    c_pad = jnp.pad(c, ((0, 8 - bsz), (0, 0)))

    x2d = x.reshape(t, d)
    for l in range(depth):
        lam_init = 0.8 - 0.6 * math.exp(-0.3 * l)
        ada = _ada(c_pad, w_ada[l], b_ada[l][None, :])[:bsz]
        ada4 = ada.reshape(bsz, N_MOD, 1, d)

        x2d = _ffn(x2d, ffn1_norm[l][None, :], ada4, 0,
                   ffn1_w_gu[l].astype(BF16), ffn1_w_down[l].astype(BF16), seq)

        q, k, v, u = _inproj(x2d, mix_norm[l][None, :], ada4, w_in[l].astype(BF16), pos, freq, sign,
                             gmat, jnp.tile(q_norm[l], 16)[None, :], jnp.tile(k_norm[l], 16)[None, :],
                             seq)
        attn = _attention(q, k, v,
                          lambda_q1[l][None, :], lambda_k1[l][None, :],
                          lambda_q2[l][None, :], lambda_k2[l][None, :],
                          subln[l][None, :], lam_init)
        cu = _conv(u.reshape(bsz, seq, CONV_WIDTH), conv_w[l], conv_b[l][None, :],
                   conv_ln_g[l][None, :], conv_ln_b[l][None, :])
        x2d = _outproj(x2d, attn.reshape(t, ATTN_WIDTH), cu.reshape(t, CONV_WIDTH),
                       w_out[l].astype(BF16), ada4, seq)

        x2d = _ffn(x2d, ffn2_norm[l][None, :], ada4, 6,
                   ffn2_w_gu[l].astype(BF16), ffn2_w_down[l].astype(BF16), seq)
    return x2d.reshape(bsz, seq, d)
```

```python
import functools
import math

import jax
import jax.numpy as jnp
from jax import lax
from jax.experimental import pallas as pl
from jax.experimental.pallas import tpu as pltpu

F32 = jnp.float32
BF16 = jnp.bfloat16

D_MODEL = 2048
ATTN_WIDTH = 1024
CONV_WIDTH = 1024
HEAD_DIM = 64
V_DIM = 128
N_HEADS = ATTN_WIDTH // V_DIM
IN_COLS = 3 * ATTN_WIDTH + 2 * CONV_WIDTH
N_IN_SEGMENTS = IN_COLS // 1024
CONV_K = 31
D_FF = 5632
ROPE_THETA = 10000.0
RMS_EPS = 1e-6
LN_EPS = 1e-5
FFN_RES = 0.5
N_MOD = 9

VMEM_LIMIT = 56 * 1024 * 1024
MASK_VALUE = -1e30
LOG2_E = math.log2(math.e)


def _silu(x):
    return x * jax.nn.sigmoid(x)


NORM_ROWS = 16


def _norm_modulate(h_ref, x_ref, nw_ref, sh_ref, sc_ref):
    nw, sh, sc1 = nw_ref[...], sh_ref[...], 1.0 + sc_ref[...]
    for r in range(x_ref.shape[0] // NORM_ROWS):
        rows = slice(r * NORM_ROWS, (r + 1) * NORM_ROWS)
        x = x_ref[rows, :]
        ms = jnp.mean(x * x, axis=-1, keepdims=True)
        y = x * lax.rsqrt(ms + RMS_EPS) * nw
        h_ref[rows, :] = (y * sc1 + sh).astype(BF16)


def _ada_kernel(c_ref, w_ref, b_ref, o_ref):
    ca = _silu(c_ref[...])
    o_ref[...] = jnp.dot(ca, w_ref[...], preferred_element_type=F32,
                         precision=lax.Precision.HIGHEST) + b_ref[...]


def _ada(c_pad, w, b):
    n = w.shape[1]
    tn = 1024
    return pl.pallas_call(
        _ada_kernel,
        grid=(n // tn,),
        in_specs=[pl.BlockSpec((8, D_MODEL), lambda j: (0, 0)),
                  pl.BlockSpec((D_MODEL, tn), lambda j: (0, j)),
                  pl.BlockSpec((1, tn), lambda j: (0, j))],
        out_specs=pl.BlockSpec((8, tn), lambda j: (0, j)),
        out_shape=jax.ShapeDtypeStruct((8, n), F32),
        compiler_params=pltpu.CompilerParams(dimension_semantics=("arbitrary",),
                                             vmem_limit_bytes=VMEM_LIMIT),
        name="ada",
    )(c_pad, w, b)


def _mod_spec(k, tiles_per_batch):
    return pl.BlockSpec((None, None, 1, D_MODEL),
                        lambda i, *_: (i // tiles_per_batch, k, 0, 0))


def _ffn_kernel(x_ref, nw_ref, sh_ref, sc_ref, gate_ref, wg_ref, wu_ref, wd_ref, o_ref, h_ref,
                *, n_f):
    f = pl.program_id(1)

    @pl.when(f == 0)
    def _():
        _norm_modulate(h_ref, x_ref, nw_ref, sh_ref, sc_ref)
        o_ref[...] = jnp.zeros_like(o_ref)

    h = h_ref[...]
    g = jnp.dot(h, wg_ref[...], preferred_element_type=F32)
    u = jnp.dot(h, wu_ref[...], preferred_element_type=F32)
    a = (_silu(g) * u).astype(BF16)
    o_ref[...] += jnp.dot(a, wd_ref[...], preferred_element_type=F32)

    @pl.when(f == n_f - 1)
    def _():
        o_ref[...] = x_ref[...] + (FFN_RES * gate_ref[...]) * o_ref[...]


def _ffn(x2d, nw, ada4, mod_base, w_gu, w_down, seq_len, tm=1024, tf=512):
    t = x2d.shape[0]
    n_f = D_FF // tf
    tpb = seq_len // tm
    row = pl.BlockSpec((tm, D_MODEL), lambda i, f: (i, 0))
    return pl.pallas_call(
        functools.partial(_ffn_kernel, n_f=n_f),
        grid=(t // tm, n_f),
        in_specs=[row,
                  pl.BlockSpec((1, D_MODEL), lambda i, f: (0, 0)),
                  _mod_spec(mod_base, tpb), _mod_spec(mod_base + 1, tpb), _mod_spec(mod_base + 2, tpb),
                  pl.BlockSpec((D_MODEL, tf), lambda i, f: (0, f)),
                  pl.BlockSpec((D_MODEL, tf), lambda i, f: (0, f + n_f)),
                  pl.BlockSpec((tf, D_MODEL), lambda i, f: (f, 0))],
        out_specs=row,
        out_shape=jax.ShapeDtypeStruct((t, D_MODEL), F32),
        scratch_shapes=[pltpu.VMEM((tm, D_MODEL), BF16)],
        compiler_params=pltpu.CompilerParams(dimension_semantics=("parallel", "arbitrary"),
                                             vmem_limit_bytes=VMEM_LIMIT),
        name="ffn",
    )(x2d, nw, ada4, ada4, ada4, w_gu, w_gu, w_down)


def _group_mean_sq(y, gmat):
    sq = y * y
    hi = sq.astype(BF16)
    lo = (sq - hi.astype(F32)).astype(BF16)
    cols = []
    for c in range(y.shape[1] // 256):
        sl = slice(c * 256, (c + 1) * 256)
        cols.append(jnp.dot(hi[:, sl], gmat, preferred_element_type=F32)
                    + jnp.dot(lo[:, sl], gmat, preferred_element_type=F32))
    return jnp.concatenate(cols, axis=1)


def _rope(y, cos, sin_signed):
    lane = lax.broadcasted_iota(jnp.int32, (y.shape[0], 128), 1)
    first_half = (lane % HEAD_DIM) < (HEAD_DIM // 2)
    cols = []
    for c in range(y.shape[1] // 128):
        t = y[:, c * 128:(c + 1) * 128]
        rot = jnp.where(first_half, pltpu.roll(t, 96, 1), pltpu.roll(t, 32, 1))
        cols.append(t * cos + rot * sin_signed)
    return jnp.concatenate(cols, axis=1)


INPROJ_ROWS = 256


def _inproj_kernel(x_ref, nw_ref, sh_ref, sc_ref, w_ref, pos_ref, freq_ref, sign_ref, gmat_ref,
                   qn_ref, kn_ref, q_ref, k_ref, v_ref, u_ref, h_ref, cos_ref, sin_ref):
    _norm_modulate(h_ref, x_ref, nw_ref, sh_ref, sc_ref)
    ang = pos_ref[...].astype(F32) * freq_ref[...]
    cos_ref[...] = jnp.cos(ang)
    sin_ref[...] = jnp.sin(ang) * sign_ref[...]

    def segment(rows, j):
        return jnp.dot(h_ref[rows, :], w_ref[:, j * 1024:(j + 1) * 1024],
                       preferred_element_type=F32)

    def qk_epilogue(res, rows, norm_w, scale):
        ms = _group_mean_sq(res, gmat_ref[...])
        y = res * lax.rsqrt(ms + RMS_EPS) * norm_w
        return (_rope(y, cos_ref[rows, :], sin_ref[rows, :]) * scale).astype(BF16)

    def store_heads(o_ref, rows, val):
        for hd in range(N_HEADS):
            o_ref[hd, rows, :] = val[:, hd * V_DIM:(hd + 1) * V_DIM]

    for r in range(x_ref.shape[0] // INPROJ_ROWS):
        rows = slice(r * INPROJ_ROWS, (r + 1) * INPROJ_ROWS)
        store_heads(q_ref, rows,
                    qk_epilogue(segment(rows, 0), rows, qn_ref[...], HEAD_DIM ** -0.5 * LOG2_E))
        store_heads(k_ref, rows, qk_epilogue(segment(rows, 1), rows, kn_ref[...], 1.0))
        store_heads(v_ref, rows, segment(rows, 2).astype(BF16))
        u_ref[rows, :] = segment(rows, 3) * jax.nn.sigmoid(segment(rows, 4))


def _inproj(x2d, nw, ada4, w_in, pos, freq, sign, gmat, qn, kn, seq_len, tm=512):
    t = x2d.shape[0]
    tpb = seq_len // tm
    const = lambda shape: pl.BlockSpec(shape, lambda i: (0, 0))
    out_row = pl.BlockSpec((tm, 1024), lambda i: (i, 0))
    head_major = pl.BlockSpec((None, N_HEADS, tm, V_DIM), lambda i: (i // tpb, 0, i % tpb, 0))
    return pl.pallas_call(
        _inproj_kernel,
        grid=(t // tm,),
        in_specs=[pl.BlockSpec((tm, D_MODEL), lambda i: (i, 0)),
                  const((1, D_MODEL)),
                  _mod_spec(3, tpb), _mod_spec(4, tpb),
                  pl.BlockSpec((D_MODEL, IN_COLS), lambda i: (0, 0), pipeline_mode=pl.Buffered(1)),
                  pl.BlockSpec((tm, 1), lambda i: (i, 0)),
                  const((1, 128)), const((1, 128)), const((256, 256)),
                  const((1, 1024)), const((1, 1024))],
        out_specs=[head_major, head_major, head_major, out_row],
        out_shape=[jax.ShapeDtypeStruct((t // seq_len, N_HEADS, seq_len, V_DIM), BF16),
                   jax.ShapeDtypeStruct((t // seq_len, N_HEADS, seq_len, V_DIM), BF16),
                   jax.ShapeDtypeStruct((t // seq_len, N_HEADS, seq_len, V_DIM), BF16),
                   jax.ShapeDtypeStruct((t, 1024), F32)],
        scratch_shapes=[pltpu.VMEM((tm, D_MODEL), BF16),
                        pltpu.VMEM((tm, 128), F32),
                        pltpu.VMEM((tm, 128), F32)],
        compiler_params=pltpu.CompilerParams(dimension_semantics=("parallel",),
                                             vmem_limit_bytes=VMEM_LIMIT),
        name="inproj",
    )(x2d, nw, ada4, ada4, w_in, pos, freq, sign, gmat, qn, kn)


ATTN_T = 512
ATTN_PV_ROWS = V_DIM + 16


def _attn_kernel(q_ref, k_ref, v_ref, lq1_ref, lk1_ref, lq2_ref, lk2_ref, subln_ref, o_ref,
                 vt_ref, qqt_ref, sa_ref, sb_ref, m_ref, acc_ref, *, lam_init):
    t = ATTN_T
    n_blocks = k_ref.shape[0] // t

    ones_row = (lax.broadcasted_iota(jnp.int32, (ATTN_PV_ROWS - V_DIM, t), 0) == 0).astype(BF16)
    for c in range(n_blocks):
        vt_ref[c, 0:V_DIM, :] = v_ref[c * t:(c + 1) * t, :].astype(F32).T.astype(BF16)
        vt_ref[c, V_DIM:ATTN_PV_ROWS, :] = ones_row

    lam = (jnp.exp(jnp.sum(lq1_ref[...] * lk1_ref[...], axis=-1, keepdims=True))
           - jnp.exp(jnp.sum(lq2_ref[...] * lk2_ref[...], axis=-1, keepdims=True))
           + lam_init)

    def build_q(i):
        qt = q_ref[i * t:(i + 1) * t, :].astype(F32).T
        feat = lax.broadcasted_iota(jnp.int32, qt.shape, 0)
        qqt_ref[i % 2] = jnp.concatenate([jnp.where(feat < HEAD_DIM, qt, 0.0),
                                          jnp.where(feat >= HEAD_DIM, qt, 0.0)],
                                         axis=1).astype(BF16)

    def scores(i, c, s_ref):
        s_ref[...] = jnp.dot(k_ref[c * t:(c + 1) * t, :], qqt_ref[i % 2],
                             preferred_element_type=F32)

    def softmax_pv(i, c, s_ref):
        par, width = i % 2, 256
        for nb in range(2 * t // width):
            cols = slice(nb * width, (nb + 1) * width)
            if c == i:
                first_query = (nb * width) & (t - 1)
                n_keys = first_query + width
                s = s_ref[0:n_keys, cols]
                key = lax.broadcasted_iota(jnp.int32, s.shape, 0)
                qry = first_query + lax.broadcasted_iota(jnp.int32, s.shape, 1)
                s = jnp.where(key <= qry, s, MASK_VALUE)
            else:
                s = s_ref[:, cols]
            cmax = jnp.max(s, axis=0, keepdims=True)
            if c == i:
                p = jnp.exp2(s - cmax)
                m_ref[par, :, cols] = cmax
                acc_ref[par, :, cols] = jnp.dot(vt_ref[c, :, 0:n_keys], p.astype(BF16),
                                                preferred_element_type=F32)
            else:
                m_old = m_ref[par, :, cols]
                m_new = jnp.maximum(m_old, cmax)
                alpha = jnp.exp2(m_old - m_new)
                p = jnp.exp2(s - m_new)
                m_ref[par, :, cols] = m_new
                acc_ref[par, :, cols] = (alpha * acc_ref[par, :, cols]
                                         + jnp.dot(vt_ref[c], p.astype(BF16),
                                                   preferred_element_type=F32))

    def finalize(i):
        par = i % 2
        o = acc_ref[par, 0:V_DIM, :] / acc_ref[par, V_DIM:V_DIM + 1, :]
        d = o[:, :t] - lam * o[:, t:]
        ms = jnp.mean(d * d, axis=0, keepdims=True)
        y = (d * lax.rsqrt(ms + RMS_EPS)).T
        o_ref[i * t:(i + 1) * t, :] = (y * subln_ref[...] * (1.0 - lam_init)).astype(BF16)

    steps = [(i, c) for i in range(n_blocks) for c in [i] + list(range(i))]
    bufs = (sa_ref, sb_ref)
    build_q(0)
    scores(0, 0, bufs[0])
    for n, (i, c) in enumerate(steps):
        if n + 1 < len(steps):
            i2, c2 = steps[n + 1]
            if i2 != i:
                build_q(i2)
            scores(i2, c2, bufs[(n + 1) % 2])
        softmax_pv(i, c, bufs[n % 2])
        if n + 1 == len(steps) or steps[n + 1][0] != i:
            finalize(i)


def _attention(q, k, v, lq1, lk1, lq2, lk2, subln, lam_init):
    b, _, s, _ = q.shape
    t = ATTN_T
    qkvspec = pl.BlockSpec((None, None, s, V_DIM), lambda bi, h: (bi, h, 0, 0))
    ospec = pl.BlockSpec((None, s, V_DIM), lambda bi, h: (bi, 0, h))
    vec = lambda n: pl.BlockSpec((1, n), lambda bi, h: (0, 0))
    return pl.pallas_call(
        functools.partial(_attn_kernel, lam_init=lam_init),
        grid=(b, N_HEADS),
        in_specs=[qkvspec, qkvspec, qkvspec, vec(HEAD_DIM), vec(HEAD_DIM), vec(HEAD_DIM),
                  vec(HEAD_DIM), vec(V_DIM)],
        out_specs=ospec,
        out_shape=jax.ShapeDtypeStruct((b, s, ATTN_WIDTH), BF16),
        scratch_shapes=[pltpu.VMEM((s // t, ATTN_PV_ROWS, t), BF16),
                        pltpu.VMEM((2, V_DIM, 2 * t), BF16),
                        pltpu.VMEM((t, 2 * t), F32),
                        pltpu.VMEM((t, 2 * t), F32),
                        pltpu.VMEM((2, 1, 2 * t), F32),
                        pltpu.VMEM((2, ATTN_PV_ROWS, 2 * t), F32)],
        compiler_params=pltpu.CompilerParams(dimension_semantics=("parallel", "parallel"),
                                             vmem_limit_bytes=VMEM_LIMIT),
        name="diffattn",
    )(q, k, v, lq1, lk1, lq2, lk2, subln)


CONV_HALO = 32
CONV_TS = 256


def _conv_kernel(u_ref, halo_ref, w_ref, b_ref, g_ref, beta_ref, o_ref, buf_ref, sh_ref, w8_ref,
                 y_ref):
    ts = CONV_TS
    i = pl.program_id(1)
    buf_ref[0:CONV_HALO, :] = jnp.where(i > 0, halo_ref[...], 0.0)
    buf_ref[CONV_HALO:CONV_HALO + ts, :] = u_ref[...]
    n_sh, piece = sh_ref.shape[1], 40
    for s in range(1, 8):
        for r0 in range(0, n_sh, piece):
            sh_ref[s - 1, r0:r0 + piece, :] = buf_ref[s + r0:s + r0 + piece, :]
    for t in range(CONV_K):
        w8_ref[t] = jnp.broadcast_to(w_ref[t:t + 1, :], (8, CONV_WIDTH))
    rc, cc = 32, 256
    first = CONV_HALO - (CONV_K - 1)

    def taps(base):
        for c in range(CONV_WIDTH // cc):
            cs = slice(c * cc, (c + 1) * cc)
            accs = [None] * (rc // 8)
            for t in range(CONV_K):
                off = first + t
                s = off % 8
                src = buf_ref if s == 0 else sh_ref.at[s - 1]
                wv = w8_ref[t, :, cs]
                for q in range(rc // 8):
                    term = src[pl.ds(base + (off - s + 8 * q), 8), cs] * wv
                    accs[q] = term if accs[q] is None else accs[q] + term
            for q in range(rc // 8):
                y_ref[pl.ds(base + 8 * q, 8), cs] = accs[q] + b_ref[:, cs]

    def row_block(r, carry):
        taps(pl.multiple_of(r * rc, rc))
        return carry

    lax.fori_loop(0, ts // rc, row_block, 0)
    y = y_ref[...]
    mu = jnp.mean(y, axis=-1, keepdims=True)
    yc = y - mu
    var = jnp.mean(yc * yc, axis=-1, keepdims=True)
    z = yc * lax.rsqrt(var + LN_EPS) * g_ref[...] + beta_ref[...]
    o_ref[...] = _silu(z).astype(BF16)


def _conv(u, w, bias, g, beta):
    b, s, c = u.shape
    ts = CONV_TS
    vec = pl.BlockSpec((1, c), lambda bi, i: (0, 0))
    per_halo = ts // CONV_HALO
    return pl.pallas_call(
        _conv_kernel,
        grid=(b, s // ts),
        in_specs=[pl.BlockSpec((None, ts, c), lambda bi, i: (bi, i, 0)),
                  pl.BlockSpec((None, CONV_HALO, c),
                               lambda bi, i: (bi, jnp.maximum(i * per_halo - 1, 0), 0)),
                  pl.BlockSpec((CONV_K, c), lambda bi, i: (0, 0)),
                  vec, vec, vec],
        out_specs=pl.BlockSpec((None, ts, c), lambda bi, i: (bi, i, 0)),
        out_shape=jax.ShapeDtypeStruct((b, s, c), BF16),
        scratch_shapes=[pltpu.VMEM((CONV_HALO + ts, c), F32),
                        pltpu.VMEM((7, CONV_HALO + ts - 8, c), F32),
                        pltpu.VMEM((CONV_K, 8, c), F32),
                        pltpu.VMEM((ts, c), F32)],
        compiler_params=pltpu.CompilerParams(dimension_semantics=("parallel", "arbitrary"),
                                             vmem_limit_bytes=VMEM_LIMIT),
        name="conv",
    )(u, u, w, bias, g, beta)


def _outproj_kernel(x_ref, a_ref, c_ref, wa_ref, wc_ref, gate_ref, o_ref):
    y = (jnp.dot(a_ref[...], wa_ref[...], preferred_element_type=F32)
         + jnp.dot(c_ref[...], wc_ref[...], preferred_element_type=F32))
    o_ref[...] = x_ref[...] + gate_ref[...] * y


def _outproj(x2d, attn, conv, w_out, ada4, seq_len, tm=512):
    t = x2d.shape[0]
    tpb = seq_len // tm
    row = pl.BlockSpec((tm, D_MODEL), lambda i: (i, 0))
    half = pl.BlockSpec((tm, 1024), lambda i: (i, 0))
    return pl.pallas_call(
        _outproj_kernel,
        grid=(t // tm,),
        in_specs=[row, half, half,
                  pl.BlockSpec((1024, D_MODEL), lambda i: (0, 0)),
                  pl.BlockSpec((1024, D_MODEL), lambda i: (1, 0)),
                  pl.BlockSpec((None, None, 1, D_MODEL), lambda i: (i // tpb, 5, 0, 0))],
        out_specs=row,
        out_shape=jax.ShapeDtypeStruct((t, D_MODEL), F32),
        compiler_params=pltpu.CompilerParams(dimension_semantics=("parallel",),
                                             vmem_limit_bytes=VMEM_LIMIT),
        name="outproj",
    )(x2d, attn, conv, w_out, w_out, ada4)


def kernel(x, c, positions, w_ada, b_ada, ffn1_norm, ffn1_w_gu, ffn1_w_down, mix_norm, w_in, q_norm, k_norm, lambda_q1, lambda_k1, lambda_q2, lambda_k2, subln, conv_w, conv_b, conv_ln_g, conv_ln_b, w_out, ffn2_norm, ffn2_w_gu, ffn2_w_down):
    bsz, seq, d = x.shape
    depth = w_ada.shape[0]
    t = bsz * seq

    inv_freq = ROPE_THETA ** (-jnp.arange(0, HEAD_DIM, 2, dtype=F32) / HEAD_DIM)
    freq = jnp.tile(inv_freq, 4)[None, :]
    sign = jnp.tile(jnp.concatenate([-jnp.ones(32, F32), jnp.ones(32, F32)]), 2)[None, :]
    grp = jnp.arange(256) // HEAD_DIM
    gmat = jnp.where(grp[:, None] == grp[None, :], 1.0 / HEAD_DIM, 0.0).astype(BF16)
    pos = positions.reshape(t, 1)
    c_pad = jnp.pad(c, ((0, 8 - bsz), (0, 0)))

    x2d = x.reshape(t, d)
    for l in range(depth):
        lam_init = 0.8 - 0.6 * math.exp(-0.3 * l)
        ada = _ada(c_pad, w_ada[l], b_ada[l][None, :])[:bsz]
        ada4 = ada.reshape(bsz, N_MOD, 1, d)

        x2d = _ffn(x2d, ffn1_norm[l][None, :], ada4, 0,
                   ffn1_w_gu[l].astype(BF16), ffn1_w_down[l].astype(BF16), seq)

        q, k, v, u = _inproj(x2d, mix_norm[l][None, :], ada4, w_in[l].astype(BF16), pos, freq, sign,
                             gmat, jnp.tile(q_norm[l], 16)[None, :], jnp.tile(k_norm[l], 16)[None, :],
                             seq)
        attn = _attention(q, k, v,
                          lambda_q1[l][None, :], lambda_k1[l][None, :],
                          lambda_q2[l][None, :], lambda_k2[l][None, :],
                          subln[l][None, :], lam_init)
        cu = _conv(u.reshape(bsz, seq, CONV_WIDTH), conv_w[l], conv_b[l][None, :],
                   conv_ln_g[l][None, :], conv_ln_b[l][None, :])
        x2d = _outproj(x2d, attn.reshape(t, ATTN_WIDTH), cu.reshape(t, CONV_WIDTH),
                       w_out[l].astype(BF16), ada4, seq)

        x2d = _ffn(x2d, ffn2_norm[l][None, :], ada4, 6,
                   ffn2_w_gu[l].astype(BF16), ffn2_w_down[l].astype(BF16), seq)
    return x2d.reshape(bsz, seq, d)
```

```python
import functools
import math

import jax
import jax.numpy as jnp
from jax import lax
from jax.experimental import pallas as pl
from jax.experimental.pallas import tpu as pltpu

F32 = jnp.float32
BF16 = jnp.bfloat16

D_MODEL = 2048
ATTN_WIDTH = 1024
CONV_WIDTH = 1024
HEAD_DIM = 64
V_DIM = 128
N_HEADS = ATTN_WIDTH // V_DIM
IN_COLS = 3 * ATTN_WIDTH + 2 * CONV_WIDTH
N_IN_SEGMENTS = IN_COLS // 1024
CONV_K = 31
D_FF = 5632
ROPE_THETA = 10000.0
RMS_EPS = 1e-6
LN_EPS = 1e-5
FFN_RES = 0.5
N_MOD = 9

VMEM_LIMIT = 56 * 1024 * 1024
FFN_VMEM_LIMIT = 60 * 1024 * 1024
MASK_VALUE = -1e30
LOG2_E = math.log2(math.e)


def _silu(x):
    return x * jax.nn.sigmoid(x)


NORM_ROWS = 16


def _norm_modulate(h_ref, x_ref, nw_ref, sh_ref, sc_ref):
    nw, sh, sc1 = nw_ref[...], sh_ref[...], 1.0 + sc_ref[...]
    for r in range(x_ref.shape[0] // NORM_ROWS):
        rows = slice(r * NORM_ROWS, (r + 1) * NORM_ROWS)
        x = x_ref[rows, :]
        ms = jnp.mean(x * x, axis=-1, keepdims=True)
        y = x * lax.rsqrt(ms + RMS_EPS) * nw
        h_ref[rows, :] = (y * sc1 + sh).astype(BF16)


def _ada_kernel(c_ref, w_ref, b_ref, o_ref):
    ca = _silu(c_ref[...])
    o_ref[...] = jnp.dot(ca, w_ref[...], preferred_element_type=F32,
                         precision=lax.Precision.HIGHEST) + b_ref[...]


def _ada(c_pad, w, b):
    n = w.shape[1]
    tn = 1024
    return pl.pallas_call(
        _ada_kernel,
        grid=(n // tn,),
        in_specs=[pl.BlockSpec((8, D_MODEL), lambda j: (0, 0)),
                  pl.BlockSpec((D_MODEL, tn), lambda j: (0, j)),
                  pl.BlockSpec((1, tn), lambda j: (0, j))],
        out_specs=pl.BlockSpec((8, tn), lambda j: (0, j)),
        out_shape=jax.ShapeDtypeStruct((8, n), F32),
        compiler_params=pltpu.CompilerParams(dimension_semantics=("arbitrary",),
                                             vmem_limit_bytes=VMEM_LIMIT),
        name="ada",
    )(c_pad, w, b)


def _mod_spec(k, tiles_per_batch):
    return pl.BlockSpec((None, None, 1, D_MODEL),
                        lambda i, *_: (i // tiles_per_batch, k, 0, 0))


FFN_TM = 1024
FFN_TF = 512


def _ffn_kernel(x_ref, nw_ref, sh_ref, sc_ref, gate_ref, wg_ref, wu_ref, wd_ref, *rest, n_f, n_cast):
    cast_src, o_ref, cast_dst, h_ref = (rest[:n_cast], rest[n_cast], rest[n_cast + 1:2 * n_cast + 1],
                                        rest[2 * n_cast + 1])
    f = pl.program_id(1)

    @pl.when(f == 0)
    def _():
        _norm_modulate(h_ref, x_ref, nw_ref, sh_ref, sc_ref)
        o_ref[...] = jnp.zeros_like(o_ref)

    h = h_ref[...]
    g = jnp.dot(h, wg_ref[...], preferred_element_type=F32)
    u = jnp.dot(h, wu_ref[...], preferred_element_type=F32)
    a = (_silu(g) * u).astype(BF16)
    o_ref[...] += jnp.dot(a, wd_ref[...], preferred_element_type=F32)

    for src, dst in zip(cast_src, cast_dst):
        dst[...] = src[...].astype(BF16)

    @pl.when(f == n_f - 1)
    def _():
        o_ref[...] = x_ref[...] + (FFN_RES * gate_ref[...]) * o_ref[...]


def _ffn(x2d, nw, ada4, mod_base, w_gu, w_down, seq_len, casts=()):
    t = x2d.shape[0]
    tm, tf = FFN_TM, FFN_TF
    n_f = D_FF // tf
    tpb = seq_len // tm
    row = pl.BlockSpec((tm, D_MODEL), lambda i, f: (i, 0))
    cast_specs = [pl.BlockSpec(shape, imap) for _, shape, imap in casts]
    outs = pl.pallas_call(
        functools.partial(_ffn_kernel, n_f=n_f, n_cast=len(casts)),
        grid=(t // tm, n_f),
        in_specs=[row,
                  pl.BlockSpec((1, D_MODEL), lambda i, f: (0, 0)),
                  _mod_spec(mod_base, tpb), _mod_spec(mod_base + 1, tpb), _mod_spec(mod_base + 2, tpb),
                  pl.BlockSpec((D_MODEL, tf), lambda i, f: (0, f)),
                  pl.BlockSpec((D_MODEL, tf), lambda i, f: (0, f + n_f)),
                  pl.BlockSpec((tf, D_MODEL), lambda i, f: (f, 0))] + cast_specs,
        out_specs=[row] + cast_specs,
        out_shape=[jax.ShapeDtypeStruct((t, D_MODEL), F32)]
                  + [jax.ShapeDtypeStruct(w.shape, BF16) for w, _, _ in casts],
        scratch_shapes=[pltpu.VMEM((tm, D_MODEL), BF16)],
        compiler_params=pltpu.CompilerParams(dimension_semantics=("parallel", "arbitrary"),
                                             vmem_limit_bytes=FFN_VMEM_LIMIT),
        name="ffn",
    )(x2d, nw, ada4, ada4, ada4, w_gu, w_gu, w_down, *[w for w, _, _ in casts])
    return outs


def _group_mean_sq(y, gmat):
    sq = y * y
    hi = sq.astype(BF16)
    lo = (sq - hi.astype(F32)).astype(BF16)
    cols = []
    for c in range(y.shape[1] // 256):
        sl = slice(c * 256, (c + 1) * 256)
        cols.append(jnp.dot(hi[:, sl], gmat, preferred_element_type=F32)
                    + jnp.dot(lo[:, sl], gmat, preferred_element_type=F32))
    return jnp.concatenate(cols, axis=1)


def _rope(y, cos, sin_signed):
    lane = lax.broadcasted_iota(jnp.int32, (y.shape[0], 128), 1)
    first_half = (lane % HEAD_DIM) < (HEAD_DIM // 2)
    cols = []
    for c in range(y.shape[1] // 128):
        t = y[:, c * 128:(c + 1) * 128]
        rot = jnp.where(first_half, pltpu.roll(t, 96, 1), pltpu.roll(t, 32, 1))
        cols.append(t * cos + rot * sin_signed)
    return jnp.concatenate(cols, axis=1)


INPROJ_ROWS = 256
CONV_HALO = 32
CONV_COLS = 128
CONV_ROWS = 64


def _mixer_in_kernel(x_ref, nw_ref, sh_ref, sc_ref, w_ref, pos_ref, freq_ref, sign_ref, gmat_ref,
                     qn_ref, kn_ref, cw_ref, cb_ref,
                     q_ref, k_ref, v_ref, y_ref,
                     h_ref, cos_ref, sin_ref, buf_ref, w8_ref, *, tiles_per_batch):
    tm = x_ref.shape[0]
    i = pl.program_id(0)

    @pl.when(i == 0)
    def _():
        buf_ref[...] = jnp.zeros_like(buf_ref)
        for t in range(CONV_K):
            w8_ref[t] = jnp.broadcast_to(cw_ref[t:t + 1, :], (8, CONV_WIDTH))

    tail = buf_ref[tm:tm + CONV_HALO, :]
    buf_ref[0:CONV_HALO, :] = jnp.where(i % tiles_per_batch == 0, 0.0, tail)

    _norm_modulate(h_ref, x_ref, nw_ref, sh_ref, sc_ref)
    ang = pos_ref[...].astype(F32) * freq_ref[...]
    cos_ref[...] = jnp.cos(ang)
    sin_ref[...] = jnp.sin(ang) * sign_ref[...]

    def segment(rows, j):
        return jnp.dot(h_ref[rows, :], w_ref[:, j * 1024:(j + 1) * 1024],
                       preferred_element_type=F32)

    def qk_epilogue(res, rows, norm_w, scale):
        ms = _group_mean_sq(res, gmat_ref[...])
        y = res * lax.rsqrt(ms + RMS_EPS) * norm_w
        return (_rope(y, cos_ref[rows, :], sin_ref[rows, :]) * scale).astype(BF16)

    def store_heads(o_ref, rows, val):
        for hd in range(N_HEADS):
            o_ref[hd, rows, :] = val[:, hd * V_DIM:(hd + 1) * V_DIM]

    row_blocks = [slice(r * INPROJ_ROWS, (r + 1) * INPROJ_ROWS) for r in range(tm // INPROJ_ROWS)]

    for r, rows in enumerate(row_blocks):
        lo = CONV_HALO + r * INPROJ_ROWS
        buf_ref[lo:lo + INPROJ_ROWS, :] = segment(rows, 3) * jax.nn.sigmoid(segment(rows, 4))

    def conv_piece(rb, c):
        cs = slice(c * CONV_COLS, (c + 1) * CONV_COLS)
        t0 = rb * CONV_ROWS
        first = CONV_HALO - (CONV_K - 1)
        out = None
        for s in range(8):
            rows_s = CONV_ROWS if s == 0 else CONV_ROWS + 8
            part = None
            for off in range(first, first + CONV_K):
                if off % 8 != s:
                    continue
                lo = t0 + off - s
                w_tile = jnp.tile(w8_ref[off - first, :, cs], (rows_s // 8, 1))
                term = buf_ref[lo:lo + rows_s, cs] * w_tile
                part = term if part is None else part + term
            shifted = part[s:s + CONV_ROWS, :]
            out = shifted if out is None else out + shifted
        y_ref[t0:t0 + CONV_ROWS, cs] = out + cb_ref[:, cs]

    pieces = [(rb, c) for c in range(CONV_WIDTH // CONV_COLS) for rb in range(tm // CONV_ROWS)]
    dots = [(rows, j) for rows in row_blocks for j in range(3)]
    per_dot = -(-len(pieces) // len(dots))
    for n, (rows, j) in enumerate(dots):
        for rb, c in pieces[n * per_dot:(n + 1) * per_dot]:
            conv_piece(rb, c)
        res = segment(rows, j)
        if j == 0:
            store_heads(q_ref, rows,
                        qk_epilogue(res, rows, qn_ref[...], HEAD_DIM ** -0.5 * LOG2_E))
        elif j == 1:
            store_heads(k_ref, rows, qk_epilogue(res, rows, kn_ref[...], 1.0))
        else:
            store_heads(v_ref, rows, res.astype(BF16))


def _mixer_in(x2d, nw, ada4, w_in, pos, freq, sign, gmat, qn, kn, conv_w, conv_b, seq_len, tm=512):
    t = x2d.shape[0]
    tpb = seq_len // tm
    const = lambda shape: pl.BlockSpec(shape, lambda i: (0, 0))
    out_row = pl.BlockSpec((tm, 1024), lambda i: (i, 0))
    head_major = pl.BlockSpec((None, N_HEADS, tm, V_DIM), lambda i: (i // tpb, 0, i % tpb, 0))
    qkv_shape = jax.ShapeDtypeStruct((t // seq_len, N_HEADS, seq_len, V_DIM), BF16)
    return pl.pallas_call(
        functools.partial(_mixer_in_kernel, tiles_per_batch=tpb),
        grid=(t // tm,),
        in_specs=[pl.BlockSpec((tm, D_MODEL), lambda i: (i, 0)),
                  const((1, D_MODEL)),
                  _mod_spec(3, tpb), _mod_spec(4, tpb),
                  pl.BlockSpec((D_MODEL, IN_COLS), lambda i: (0, 0), pipeline_mode=pl.Buffered(1)),
                  pl.BlockSpec((tm, 1), lambda i: (i, 0)),
                  const((1, 128)), const((1, 128)), const((256, 256)),
                  const((1, 1024)), const((1, 1024)),
                  const((CONV_K, CONV_WIDTH)), const((1, CONV_WIDTH))],
        out_specs=[head_major, head_major, head_major, out_row],
        out_shape=[qkv_shape, qkv_shape, qkv_shape, jax.ShapeDtypeStruct((t, CONV_WIDTH), F32)],
        scratch_shapes=[pltpu.VMEM((tm, D_MODEL), BF16),
                        pltpu.VMEM((tm, 128), F32),
                        pltpu.VMEM((tm, 128), F32),
                        pltpu.VMEM((CONV_HALO + tm, CONV_WIDTH), F32),
                        pltpu.VMEM((CONV_K, 8, CONV_WIDTH), F32)],
        compiler_params=pltpu.CompilerParams(dimension_semantics=("arbitrary",),
                                             vmem_limit_bytes=VMEM_LIMIT),
        name="mixer_in",
    )(x2d, nw, ada4, ada4, w_in, pos, freq, sign, gmat, qn, kn, conv_w, conv_b)


ATTN_T = 512
ATTN_PV_ROWS = V_DIM + 16


def _attn_kernel(q_ref, k_ref, v_ref, lq1_ref, lk1_ref, lq2_ref, lk2_ref, subln_ref, o_ref,
                 vt_ref, qqt_ref, sa_ref, sb_ref, m_ref, acc_ref, *, lam_init):
    t = ATTN_T
    n_blocks = k_ref.shape[0] // t

    ones_row = (lax.broadcasted_iota(jnp.int32, (ATTN_PV_ROWS - V_DIM, t), 0) == 0).astype(BF16)
    for c in range(n_blocks):
        vt_ref[c, 0:V_DIM, :] = v_ref[c * t:(c + 1) * t, :].astype(F32).T.astype(BF16)
        vt_ref[c, V_DIM:ATTN_PV_ROWS, :] = ones_row

    lam = (jnp.exp(jnp.sum(lq1_ref[...] * lk1_ref[...], axis=-1, keepdims=True))
           - jnp.exp(jnp.sum(lq2_ref[...] * lk2_ref[...], axis=-1, keepdims=True))
           + lam_init)

    def build_q(i):
        qt = q_ref[i * t:(i + 1) * t, :].astype(F32).T
        feat = lax.broadcasted_iota(jnp.int32, qt.shape, 0)
        qqt_ref[i % 2] = jnp.concatenate([jnp.where(feat < HEAD_DIM, qt, 0.0),
                                          jnp.where(feat >= HEAD_DIM, qt, 0.0)],
                                         axis=1).astype(BF16)

    def scores(i, c, s_ref):
        s_ref[...] = jnp.dot(k_ref[c * t:(c + 1) * t, :], qqt_ref[i % 2],
                             preferred_element_type=F32)

    def softmax_pv(i, c, s_ref):
        par, width = i % 2, 256
        for nb in range(2 * t // width):
            cols = slice(nb * width, (nb + 1) * width)
            if c == i:
                first_query = (nb * width) & (t - 1)
                n_keys = first_query + width
                s = s_ref[0:n_keys, cols]
                key = lax.broadcasted_iota(jnp.int32, s.shape, 0)
                qry = first_query + lax.broadcasted_iota(jnp.int32, s.shape, 1)
                s = jnp.where(key <= qry, s, MASK_VALUE)
            else:
                s = s_ref[:, cols]
            cmax = jnp.max(s, axis=0, keepdims=True)
            if c == i:
                p = jnp.exp2(s - cmax)
                m_ref[par, :, cols] = cmax
                acc_ref[par, :, cols] = jnp.dot(vt_ref[c, :, 0:n_keys], p.astype(BF16),
                                                preferred_element_type=F32)
            else:
                m_old = m_ref[par, :, cols]
                m_new = jnp.maximum(m_old, cmax)
                alpha = jnp.exp2(m_old - m_new)
                p = jnp.exp2(s - m_new)
                m_ref[par, :, cols] = m_new
                acc_ref[par, :, cols] = (alpha * acc_ref[par, :, cols]
                                         + jnp.dot(vt_ref[c], p.astype(BF16),
                                                   preferred_element_type=F32))

    def finalize(i):
        par = i % 2
        o = acc_ref[par, 0:V_DIM, :] / acc_ref[par, V_DIM:V_DIM + 1, :]
        d = o[:, :t] - lam * o[:, t:]
        ms = jnp.mean(d * d, axis=0, keepdims=True)
        y = (d * lax.rsqrt(ms + RMS_EPS)).T
        o_ref[i * t:(i + 1) * t, :] = (y * subln_ref[...] * (1.0 - lam_init)).astype(BF16)

    steps = [(i, c) for i in range(n_blocks) for c in [i] + list(range(i))]
    bufs = (sa_ref, sb_ref)
    build_q(0)
    scores(0, 0, bufs[0])
    for n, (i, c) in enumerate(steps):
        if n + 1 < len(steps):
            i2, c2 = steps[n + 1]
            if i2 != i:
                build_q(i2)
            scores(i2, c2, bufs[(n + 1) % 2])
        softmax_pv(i, c, bufs[n % 2])
        if n + 1 == len(steps) or steps[n + 1][0] != i:
            finalize(i)


def _attention(q, k, v, lq1, lk1, lq2, lk2, subln, lam_init):
    b, _, s, _ = q.shape
    t = ATTN_T
    qkvspec = pl.BlockSpec((None, None, s, V_DIM), lambda bi, h: (bi, h, 0, 0))
    ospec = pl.BlockSpec((None, s, V_DIM), lambda bi, h: (bi, 0, h))
    vec = lambda n: pl.BlockSpec((1, n), lambda bi, h: (0, 0))
    return pl.pallas_call(
        functools.partial(_attn_kernel, lam_init=lam_init),
        grid=(b, N_HEADS),
        in_specs=[qkvspec, qkvspec, qkvspec, vec(HEAD_DIM), vec(HEAD_DIM), vec(HEAD_DIM),
                  vec(HEAD_DIM), vec(V_DIM)],
        out_specs=ospec,
        out_shape=jax.ShapeDtypeStruct((b, s, ATTN_WIDTH), BF16),
        scratch_shapes=[pltpu.VMEM((s // t, ATTN_PV_ROWS, t), BF16),
                        pltpu.VMEM((2, V_DIM, 2 * t), BF16),
                        pltpu.VMEM((t, 2 * t), F32),
                        pltpu.VMEM((t, 2 * t), F32),
                        pltpu.VMEM((2, 1, 2 * t), F32),
                        pltpu.VMEM((2, ATTN_PV_ROWS, 2 * t), F32)],
        compiler_params=pltpu.CompilerParams(dimension_semantics=("parallel", "parallel"),
                                             vmem_limit_bytes=VMEM_LIMIT),
        name="diffattn",
    )(q, k, v, lq1, lk1, lq2, lk2, subln)


def _outproj_kernel(x_ref, a_ref, y_ref, cg_ref, cbeta_ref, wa_ref, wc_ref, gate_ref, o_ref, c_ref):
    for r in range(y_ref.shape[0] // NORM_ROWS):
        rows = slice(r * NORM_ROWS, (r + 1) * NORM_ROWS)
        y = y_ref[rows, :]
        mu = jnp.mean(y, axis=-1, keepdims=True)
        yc = y - mu
        var = jnp.mean(yc * yc, axis=-1, keepdims=True)
        z = yc * lax.rsqrt(var + LN_EPS) * cg_ref[...] + cbeta_ref[...]
        c_ref[rows, :] = _silu(z).astype(BF16)
    out = (jnp.dot(a_ref[...], wa_ref[...], preferred_element_type=F32)
           + jnp.dot(c_ref[...], wc_ref[...], preferred_element_type=F32))
    o_ref[...] = x_ref[...] + gate_ref[...] * out


def _outproj(x2d, attn, conv_y, conv_g, conv_beta, w_out, ada4, seq_len, tm=512):
    t = x2d.shape[0]
    tpb = seq_len // tm
    row = pl.BlockSpec((tm, D_MODEL), lambda i: (i, 0))
    half = pl.BlockSpec((tm, 1024), lambda i: (i, 0))
    vec = pl.BlockSpec((1, CONV_WIDTH), lambda i: (0, 0))
    return pl.pallas_call(
        _outproj_kernel,
        grid=(t // tm,),
        in_specs=[row, half, half, vec, vec,
                  pl.BlockSpec((1024, D_MODEL), lambda i: (0, 0)),
                  pl.BlockSpec((1024, D_MODEL), lambda i: (1, 0)),
                  pl.BlockSpec((None, None, 1, D_MODEL), lambda i: (i // tpb, 5, 0, 0))],
        out_specs=row,
        out_shape=jax.ShapeDtypeStruct((t, D_MODEL), F32),
        scratch_shapes=[pltpu.VMEM((tm, CONV_WIDTH), BF16)],
        compiler_params=pltpu.CompilerParams(dimension_semantics=("parallel",),
                                             vmem_limit_bytes=VMEM_LIMIT),
        name="outproj",
    )(x2d, attn, conv_y, conv_g, conv_beta, w_out, w_out, ada4)


def kernel(x, c, positions, w_ada, b_ada, ffn1_norm, ffn1_w_gu, ffn1_w_down, mix_norm, w_in, q_norm, k_norm, lambda_q1, lambda_k1, lambda_q2, lambda_k2, subln, conv_w, conv_b, conv_ln_g, conv_ln_b, w_out, ffn2_norm, ffn2_w_gu, ffn2_w_down):
    bsz, seq, d = x.shape
    depth = w_ada.shape[0]
    t = bsz * seq

    inv_freq = ROPE_THETA ** (-jnp.arange(0, HEAD_DIM, 2, dtype=F32) / HEAD_DIM)
    freq = jnp.tile(inv_freq, 4)[None, :]
    sign = jnp.tile(jnp.concatenate([-jnp.ones(32, F32), jnp.ones(32, F32)]), 2)[None, :]
    grp = jnp.arange(256) // HEAD_DIM
    gmat = jnp.where(grp[:, None] == grp[None, :], 1.0 / HEAD_DIM, 0.0).astype(BF16)
    pos = positions.reshape(t, 1)
    c_pad = jnp.pad(c, ((0, 8 - bsz), (0, 0)))

    x2d = x.reshape(t, d)
    for l in range(depth):
        lam_init = 0.8 - 0.6 * math.exp(-0.3 * l)
        ada = _ada(c_pad, w_ada[l], b_ada[l][None, :])[:bsz]
        ada4 = ada.reshape(bsz, N_MOD, 1, d)

        n_i, n_f = t // FFN_TM, D_FF // FFN_TF
        casts = ((ffn2_w_gu[l], (d // n_i, 2 * D_FF // n_f), lambda i, f: (i, f)),
                 (ffn2_w_down[l], (D_FF // n_f, d // n_i), lambda i, f: (f, i)),
                 (w_in[l], (d // n_i, 512), lambda i, f: (i, jnp.minimum(f, IN_COLS // 512 - 1))),
                 (w_out[l], (d // n_i, 256), lambda i, f: (i, jnp.minimum(f, d // 256 - 1))))
        x2d, w_gu2, w_down2, w_in_bf, w_out_bf = _ffn(
            x2d, ffn1_norm[l][None, :], ada4, 0,
            ffn1_w_gu[l].astype(BF16), ffn1_w_down[l].astype(BF16), seq, casts)

        q, k, v, cy = _mixer_in(x2d, mix_norm[l][None, :], ada4, w_in_bf, pos, freq, sign,
                                gmat, jnp.tile(q_norm[l], 16)[None, :],
                                jnp.tile(k_norm[l], 16)[None, :],
                                conv_w[l], conv_b[l][None, :], seq)
        attn = _attention(q, k, v,
                          lambda_q1[l][None, :], lambda_k1[l][None, :],
                          lambda_q2[l][None, :], lambda_k2[l][None, :],
                          subln[l][None, :], lam_init)
        x2d = _outproj(x2d, attn.reshape(t, ATTN_WIDTH), cy, conv_ln_g[l][None, :],
                       conv_ln_b[l][None, :], w_out_bf, ada4, seq)

        x2d, = _ffn(x2d, ffn2_norm[l][None, :], ada4, 6, w_gu2, w_down2, seq)
    return x2d.reshape(bsz, seq, d)
```

```python
import functools
import math

import jax
import jax.numpy as jnp
from jax import lax
from jax.experimental import pallas as pl
from jax.experimental.pallas import tpu as pltpu

F32 = jnp.float32
BF16 = jnp.bfloat16

D_MODEL = 2048
ATTN_WIDTH = 1024
CONV_WIDTH = 1024
HEAD_DIM = 64
V_DIM = 128
N_HEADS = ATTN_WIDTH // V_DIM
IN_COLS = 3 * ATTN_WIDTH + 2 * CONV_WIDTH
N_IN_SEGMENTS = IN_COLS // 1024
CONV_K = 31
D_FF = 5632
ROPE_THETA = 10000.0
RMS_EPS = 1e-6
LN_EPS = 1e-5
FFN_RES = 0.5
N_MOD = 9

VMEM_LIMIT = 56 * 1024 * 1024
FFN_VMEM_LIMIT = 60 * 1024 * 1024
MASK_VALUE = -1e30
LOG2_E = math.log2(math.e)


def _silu(x):
    return x * jax.nn.sigmoid(x)


NORM_ROWS = 16


def _norm_modulate(h_ref, x_ref, nw_ref, sh_ref, sc_ref):
    nw, sh, sc1 = nw_ref[...], sh_ref[...], 1.0 + sc_ref[...]
    for r in range(x_ref.shape[0] // NORM_ROWS):
        rows = slice(r * NORM_ROWS, (r + 1) * NORM_ROWS)
        x = x_ref[rows, :]
        ms = jnp.mean(x * x, axis=-1, keepdims=True)
        y = x * lax.rsqrt(ms + RMS_EPS) * nw
        h_ref[rows, :] = (y * sc1 + sh).astype(BF16)


ADA_ROWS = 16


def _ada_kernel(c_ref, w_ref, b_ref, o_ref):
    ca = _silu(c_ref[...])
    hi = ca.astype(BF16)
    lo = (ca - hi.astype(F32)).astype(BF16)
    w = w_ref[...].astype(BF16)
    o_ref[...] = (jnp.dot(hi, w, preferred_element_type=F32)
                  + jnp.dot(lo, w, preferred_element_type=F32) + b_ref[...])


def _ada(c_pad, w, b):
    n = w.shape[1]
    tn = 1024
    return pl.pallas_call(
        _ada_kernel,
        grid=(n // tn,),
        in_specs=[pl.BlockSpec((ADA_ROWS, D_MODEL), lambda j: (0, 0)),
                  pl.BlockSpec((D_MODEL, tn), lambda j: (0, j)),
                  pl.BlockSpec((1, tn), lambda j: (0, j))],
        out_specs=pl.BlockSpec((ADA_ROWS, tn), lambda j: (0, j)),
        out_shape=jax.ShapeDtypeStruct((ADA_ROWS, n), F32),
        compiler_params=pltpu.CompilerParams(dimension_semantics=("arbitrary",),
                                             vmem_limit_bytes=VMEM_LIMIT),
        name="ada",
    )(c_pad, w, b)


def _mod_spec(k, tiles_per_batch):
    return pl.BlockSpec((None, None, 1, D_MODEL),
                        lambda i, *_: (i // tiles_per_batch, k, 0, 0))


FFN_TM = 1024
FFN_TF = 512


def _ffn_kernel(x_ref, nw_ref, sh_ref, sc_ref, gate_ref, wg_ref, wu_ref, wd_ref, *rest, n_cast):
    cast_src, o_ref, cast_dst, h_ref = (rest[:n_cast], rest[n_cast], rest[n_cast + 1:2 * n_cast + 1],
                                        rest[2 * n_cast + 1])
    f = pl.program_id(1)

    @pl.when(f == 0)
    def _():
        _norm_modulate(h_ref, x_ref, nw_ref, sh_ref, sc_ref)
        o_ref[...] = x_ref[...]

    h = h_ref[...]
    g = jnp.dot(h, wg_ref[...], preferred_element_type=F32)
    u = jnp.dot(h, wu_ref[...], preferred_element_type=F32)
    a = (_silu(g) * u).astype(BF16)
    o_ref[...] += (FFN_RES * gate_ref[...]) * jnp.dot(a, wd_ref[...], preferred_element_type=F32)

    for src, dst in zip(cast_src, cast_dst):
        dst[...] = src[...].astype(BF16)


def _ffn(x2d, nw, ada4, mod_base, w_gu, w_down, seq_len, casts=()):
    t = x2d.shape[0]
    tm, tf = FFN_TM, FFN_TF
    n_f = D_FF // tf
    tpb = seq_len // tm
    row = pl.BlockSpec((tm, D_MODEL), lambda i, f: (i, 0))
    cast_specs = [pl.BlockSpec(shape, imap) for _, shape, imap in casts]
    outs = pl.pallas_call(
        functools.partial(_ffn_kernel, n_cast=len(casts)),
        grid=(t // tm, n_f),
        in_specs=[row,
                  pl.BlockSpec((1, D_MODEL), lambda i, f: (0, 0)),
                  _mod_spec(mod_base, tpb), _mod_spec(mod_base + 1, tpb), _mod_spec(mod_base + 2, tpb),
                  pl.BlockSpec((D_MODEL, tf), lambda i, f: (0, f)),
                  pl.BlockSpec((D_MODEL, tf), lambda i, f: (0, f + n_f)),
                  pl.BlockSpec((tf, D_MODEL), lambda i, f: (f, 0))] + cast_specs,
        out_specs=[row] + cast_specs,
        out_shape=[jax.ShapeDtypeStruct((t, D_MODEL), F32)]
                  + [jax.ShapeDtypeStruct(w.shape, BF16) for w, _, _ in casts],
        scratch_shapes=[pltpu.VMEM((tm, D_MODEL), BF16)],
        compiler_params=pltpu.CompilerParams(dimension_semantics=("parallel", "arbitrary"),
                                             vmem_limit_bytes=FFN_VMEM_LIMIT),
        name="ffn",
    )(x2d, nw, ada4, ada4, ada4, w_gu, w_gu, w_down, *[w for w, _, _ in casts])
    return outs


def _group_mean_sq(y, gmat):
    sq = y * y
    hi = sq.astype(BF16)
    lo = (sq - hi.astype(F32)).astype(BF16)
    cols = []
    for c in range(y.shape[1] // 256):
        sl = slice(c * 256, (c + 1) * 256)
        cols.append(jnp.dot(hi[:, sl], gmat, preferred_element_type=F32)
                    + jnp.dot(lo[:, sl], gmat, preferred_element_type=F32))
    return jnp.concatenate(cols, axis=1)


def _rope(y, cos, sin_signed):
    lane = lax.broadcasted_iota(jnp.int32, (y.shape[0], 128), 1)
    first_half = (lane % HEAD_DIM) < (HEAD_DIM // 2)
    cols = []
    for c in range(y.shape[1] // 128):
        t = y[:, c * 128:(c + 1) * 128]
        rot = jnp.where(first_half, pltpu.roll(t, 96, 1), pltpu.roll(t, 32, 1))
        cols.append(t * cos + rot * sin_signed)
    return jnp.concatenate(cols, axis=1)


INPROJ_ROWS = 256
CONV_HALO = 32
CONV_COLS = 128
CONV_ROWS = 64


def _mixer_in_kernel(x_ref, nw_ref, sh_ref, sc_ref, w_ref, pos_ref, freq_ref, sign_ref, gmat_ref,
                     qn_ref, kn_ref, cw_ref, cb_ref,
                     q_ref, k_ref, v_ref, y_ref,
                     h_ref, cos_ref, sin_ref, buf_ref, w8_ref, *, tiles_per_batch):
    tm = x_ref.shape[0]
    i = pl.program_id(0)

    @pl.when(i == 0)
    def _():
        buf_ref[...] = jnp.zeros_like(buf_ref)
        for t in range(CONV_K):
            w8_ref[t] = jnp.broadcast_to(cw_ref[t:t + 1, :], (8, CONV_WIDTH))

    tail = buf_ref[tm:tm + CONV_HALO, :]
    buf_ref[0:CONV_HALO, :] = jnp.where(i % tiles_per_batch == 0, 0.0, tail)

    _norm_modulate(h_ref, x_ref, nw_ref, sh_ref, sc_ref)
    ang = pos_ref[...].astype(F32) * freq_ref[...]
    cos_ref[...] = jnp.cos(ang)
    sin_ref[...] = jnp.sin(ang) * sign_ref[...]

    def segment(rows, j):
        return jnp.dot(h_ref[rows, :], w_ref[:, j * 1024:(j + 1) * 1024],
                       preferred_element_type=F32)

    def qk_epilogue(res, rows, norm_w, scale):
        ms = _group_mean_sq(res, gmat_ref[...])
        y = res * lax.rsqrt(ms + RMS_EPS) * norm_w
        return (_rope(y, cos_ref[rows, :], sin_ref[rows, :]) * scale).astype(BF16)

    def store_heads(o_ref, rows, val):
        for hd in range(N_HEADS):
            o_ref[hd, rows, :] = val[:, hd * V_DIM:(hd + 1) * V_DIM]

    row_blocks = [slice(r * INPROJ_ROWS, (r + 1) * INPROJ_ROWS) for r in range(tm // INPROJ_ROWS)]

    for r, rows in enumerate(row_blocks):
        lo = CONV_HALO + r * INPROJ_ROWS
        buf_ref[lo:lo + INPROJ_ROWS, :] = segment(rows, 3) * jax.nn.sigmoid(segment(rows, 4))

    def conv_piece(rb, c):
        cs = slice(c * CONV_COLS, (c + 1) * CONV_COLS)
        t0 = rb * CONV_ROWS
        first = CONV_HALO - (CONV_K - 1)
        out = None
        for s in range(8):
            rows_s = CONV_ROWS if s == 0 else CONV_ROWS + 8
            part = None
            for off in range(first, first + CONV_K):
                if off % 8 != s:
                    continue
                lo = t0 + off - s
                w_tile = jnp.tile(w8_ref[off - first, :, cs], (rows_s // 8, 1))
                term = buf_ref[lo:lo + rows_s, cs] * w_tile
                part = term if part is None else part + term
            shifted = part[s:s + CONV_ROWS, :]
            out = shifted if out is None else out + shifted
        y_ref[t0:t0 + CONV_ROWS, cs] = out + cb_ref[:, cs]

    pieces = [(rb, c) for c in range(CONV_WIDTH // CONV_COLS) for rb in range(tm // CONV_ROWS)]
    dots = [(rows, j) for rows in row_blocks for j in range(3)]
    per_dot = -(-len(pieces) // len(dots))
    for n, (rows, j) in enumerate(dots):
        for rb, c in pieces[n * per_dot:(n + 1) * per_dot]:
            conv_piece(rb, c)
        res = segment(rows, j)
        if j == 0:
            store_heads(q_ref, rows,
                        qk_epilogue(res, rows, qn_ref[...], HEAD_DIM ** -0.5 * LOG2_E))
        elif j == 1:
            store_heads(k_ref, rows, qk_epilogue(res, rows, kn_ref[...], 1.0))
        else:
            store_heads(v_ref, rows, res.astype(BF16))


def _mixer_in(x2d, nw, ada4, w_in, pos, freq, sign, gmat, qn, kn, conv_w, conv_b, seq_len, tm=512):
    t = x2d.shape[0]
    tpb = seq_len // tm
    const = lambda shape: pl.BlockSpec(shape, lambda i: (0, 0))
    out_row = pl.BlockSpec((tm, 1024), lambda i: (i, 0))
    head_major = pl.BlockSpec((None, N_HEADS, tm, V_DIM), lambda i: (i // tpb, 0, i % tpb, 0))
    qkv_shape = jax.ShapeDtypeStruct((t // seq_len, N_HEADS, seq_len, V_DIM), BF16)
    return pl.pallas_call(
        functools.partial(_mixer_in_kernel, tiles_per_batch=tpb),
        grid=(t // tm,),
        in_specs=[pl.BlockSpec((tm, D_MODEL), lambda i: (i, 0)),
                  const((1, D_MODEL)),
                  _mod_spec(3, tpb), _mod_spec(4, tpb),
                  pl.BlockSpec((D_MODEL, IN_COLS), lambda i: (0, 0), pipeline_mode=pl.Buffered(1)),
                  pl.BlockSpec((tm, 1), lambda i: (i, 0)),
                  const((1, 128)), const((1, 128)), const((256, 256)),
                  const((1, 1024)), const((1, 1024)),
                  const((CONV_K, CONV_WIDTH)), const((1, CONV_WIDTH))],
        out_specs=[head_major, head_major, head_major, out_row],
        out_shape=[qkv_shape, qkv_shape, qkv_shape, jax.ShapeDtypeStruct((t, CONV_WIDTH), F32)],
        scratch_shapes=[pltpu.VMEM((tm, D_MODEL), BF16),
                        pltpu.VMEM((tm, 128), F32),
                        pltpu.VMEM((tm, 128), F32),
                        pltpu.VMEM((CONV_HALO + tm, CONV_WIDTH), F32),
                        pltpu.VMEM((CONV_K, 8, CONV_WIDTH), F32)],
        compiler_params=pltpu.CompilerParams(dimension_semantics=("arbitrary",),
                                             vmem_limit_bytes=VMEM_LIMIT),
        name="mixer_in",
    )(x2d, nw, ada4, ada4, w_in, pos, freq, sign, gmat, qn, kn, conv_w, conv_b)


ATTN_T = 512
ATTN_PV_ROWS = V_DIM + 16


def _attn_kernel(q_ref, k_ref, v_ref, lq1_ref, lk1_ref, lq2_ref, lk2_ref, subln_ref, o_ref,
                 vt_ref, qqt_ref, sa_ref, sb_ref, sc_ref, m_ref, acc_ref, *, lam_init):
    t = ATTN_T
    n_blocks = k_ref.shape[0] // t

    ones_row = (lax.broadcasted_iota(jnp.int32, (ATTN_PV_ROWS - V_DIM, t), 0) == 0).astype(BF16)
    for c in range(n_blocks):
        vt_ref[c, 0:V_DIM, :] = v_ref[c * t:(c + 1) * t, :].astype(F32).T.astype(BF16)
        vt_ref[c, V_DIM:ATTN_PV_ROWS, :] = ones_row

    lam = (jnp.exp(jnp.sum(lq1_ref[...] * lk1_ref[...], axis=-1, keepdims=True))
           - jnp.exp(jnp.sum(lq2_ref[...] * lk2_ref[...], axis=-1, keepdims=True))
           + lam_init)

    def build_q(i):
        qt = q_ref[i * t:(i + 1) * t, :].astype(F32).T
        feat = lax.broadcasted_iota(jnp.int32, qt.shape, 0)
        qqt_ref[i % 2] = jnp.concatenate([jnp.where(feat < HEAD_DIM, qt, 0.0),
                                          jnp.where(feat >= HEAD_DIM, qt, 0.0)],
                                         axis=1).astype(BF16)

    def scores(i, c, s_ref):
        s_ref[...] = jnp.dot(k_ref[c * t:(c + 1) * t, :], qqt_ref[i % 2],
                             preferred_element_type=F32)

    def softmax_pv(i, c, s_ref):
        par, width = i % 2, 256
        for nb in range(2 * t // width):
            cols = slice(nb * width, (nb + 1) * width)
            if c == i:
                first_query = (nb * width) & (t - 1)
                n_keys = first_query + width
                s = s_ref[0:n_keys, cols]
                key = lax.broadcasted_iota(jnp.int32, s.shape, 0)
                qry = first_query + lax.broadcasted_iota(jnp.int32, s.shape, 1)
                s = jnp.where(key <= qry, s, MASK_VALUE)
            else:
                s = s_ref[:, cols]
            cmax = jnp.max(s, axis=0, keepdims=True)
            if c == i:
                p = jnp.exp2(s - cmax)
                m_ref[par, :, cols] = cmax
                acc_ref[par, :, cols] = jnp.dot(vt_ref[c, :, 0:n_keys], p.astype(BF16),
                                                preferred_element_type=F32)
            else:
                m_old = m_ref[par, :, cols]
                m_new = jnp.maximum(m_old, cmax)
                alpha = jnp.exp2(m_old - m_new)
                p = jnp.exp2(s - m_new)
                m_ref[par, :, cols] = m_new
                acc_ref[par, :, cols] = (alpha * acc_ref[par, :, cols]
                                         + jnp.dot(vt_ref[c], p.astype(BF16),
                                                   preferred_element_type=F32))

    def finalize(i):
        par = i % 2
        o = acc_ref[par, 0:V_DIM, :] / acc_ref[par, V_DIM:V_DIM + 1, :]
        d = o[:, :t] - lam * o[:, t:]
        ms = jnp.mean(d * d, axis=0, keepdims=True)
        y = (d * lax.rsqrt(ms + RMS_EPS)).T
        o_ref[i * t:(i + 1) * t, :] = (y * subln_ref[...] * (1.0 - lam_init)).astype(BF16)

    steps = [(i, c) for i in range(n_blocks) for c in [i] + list(range(i))]
    bufs = (sa_ref, sb_ref, sc_ref)
    ahead = len(bufs) - 1

    def issue(n):
        i2, c2 = steps[n]
        if c2 == i2:
            build_q(i2)
        scores(i2, c2, bufs[n % len(bufs)])

    for n in range(min(ahead, len(steps))):
        issue(n)
    for n, (i, c) in enumerate(steps):
        if n + ahead < len(steps):
            issue(n + ahead)
        softmax_pv(i, c, bufs[n % len(bufs)])
        if n + 1 == len(steps) or steps[n + 1][0] != i:
            finalize(i)


def _attention(q, k, v, lq1, lk1, lq2, lk2, subln, lam_init):
    b, _, s, _ = q.shape
    t = ATTN_T
    qkvspec = pl.BlockSpec((None, None, s, V_DIM), lambda bi, h: (bi, h, 0, 0))
    ospec = pl.BlockSpec((None, s, V_DIM), lambda bi, h: (bi, 0, h))
    vec = lambda n: pl.BlockSpec((1, n), lambda bi, h: (0, 0))
    return pl.pallas_call(
        functools.partial(_attn_kernel, lam_init=lam_init),
        grid=(b, N_HEADS),
        in_specs=[qkvspec, qkvspec, qkvspec, vec(HEAD_DIM), vec(HEAD_DIM), vec(HEAD_DIM),
                  vec(HEAD_DIM), vec(V_DIM)],
        out_specs=ospec,
        out_shape=jax.ShapeDtypeStruct((b, s, ATTN_WIDTH), BF16),
        scratch_shapes=[pltpu.VMEM((s // t, ATTN_PV_ROWS, t), BF16),
                        pltpu.VMEM((2, V_DIM, 2 * t), BF16),
                        pltpu.VMEM((t, 2 * t), F32),
                        pltpu.VMEM((t, 2 * t), F32),
                        pltpu.VMEM((t, 2 * t), F32),
                        pltpu.VMEM((2, 1, 2 * t), F32),
                        pltpu.VMEM((2, ATTN_PV_ROWS, 2 * t), F32)],
        compiler_params=pltpu.CompilerParams(dimension_semantics=("parallel", "parallel"),
                                             vmem_limit_bytes=VMEM_LIMIT),
        name="diffattn",
    )(q, k, v, lq1, lk1, lq2, lk2, subln)


def _outproj_kernel(x_ref, a_ref, y_ref, cg_ref, cbeta_ref, wa_ref, wc_ref, gate_ref, o_ref, c_ref):
    for r in range(y_ref.shape[0] // NORM_ROWS):
        rows = slice(r * NORM_ROWS, (r + 1) * NORM_ROWS)
        y = y_ref[rows, :]
        mu = jnp.mean(y, axis=-1, keepdims=True)
        yc = y - mu
        var = jnp.mean(yc * yc, axis=-1, keepdims=True)
        z = yc * lax.rsqrt(var + LN_EPS) * cg_ref[...] + cbeta_ref[...]
        c_ref[rows, :] = _silu(z).astype(BF16)
    out = (jnp.dot(a_ref[...], wa_ref[...], preferred_element_type=F32)
           + jnp.dot(c_ref[...], wc_ref[...], preferred_element_type=F32))
    o_ref[...] = x_ref[...] + gate_ref[...] * out


def _outproj(x2d, attn, conv_y, conv_g, conv_beta, w_out, ada4, seq_len, tm=512):
    t = x2d.shape[0]
    tpb = seq_len // tm
    row = pl.BlockSpec((tm, D_MODEL), lambda i: (i, 0))
    half = pl.BlockSpec((tm, 1024), lambda i: (i, 0))
    vec = pl.BlockSpec((1, CONV_WIDTH), lambda i: (0, 0))
    return pl.pallas_call(
        _outproj_kernel,
        grid=(t // tm,),
        in_specs=[row, half, half, vec, vec,
                  pl.BlockSpec((1024, D_MODEL), lambda i: (0, 0)),
                  pl.BlockSpec((1024, D_MODEL), lambda i: (1, 0)),
                  pl.BlockSpec((None, None, 1, D_MODEL), lambda i: (i // tpb, 5, 0, 0))],
        out_specs=row,
        out_shape=jax.ShapeDtypeStruct((t, D_MODEL), F32),
        scratch_shapes=[pltpu.VMEM((tm, CONV_WIDTH), BF16)],
        compiler_params=pltpu.CompilerParams(dimension_semantics=("parallel",),
                                             vmem_limit_bytes=VMEM_LIMIT),
        name="outproj",
    )(x2d, attn, conv_y, conv_g, conv_beta, w_out, w_out, ada4)


def kernel(x, c, positions, w_ada, b_ada, ffn1_norm, ffn1_w_gu, ffn1_w_down, mix_norm, w_in, q_norm, k_norm, lambda_q1, lambda_k1, lambda_q2, lambda_k2, subln, conv_w, conv_b, conv_ln_g, conv_ln_b, w_out, ffn2_norm, ffn2_w_gu, ffn2_w_down):
    bsz, seq, d = x.shape
    depth = w_ada.shape[0]
    t = bsz * seq

    inv_freq = ROPE_THETA ** (-jnp.arange(0, HEAD_DIM, 2, dtype=F32) / HEAD_DIM)
    freq = jnp.tile(inv_freq, 4)[None, :]
    sign = jnp.tile(jnp.concatenate([-jnp.ones(32, F32), jnp.ones(32, F32)]), 2)[None, :]
    grp = jnp.arange(256) // HEAD_DIM
    gmat = jnp.where(grp[:, None] == grp[None, :], 1.0 / HEAD_DIM, 0.0).astype(BF16)
    pos = positions.reshape(t, 1)
    c_pad = jnp.pad(c, ((0, ADA_ROWS - bsz), (0, 0)))

    x2d = x.reshape(t, d)
    for l in range(depth):
        lam_init = 0.8 - 0.6 * math.exp(-0.3 * l)
        ada = _ada(c_pad, w_ada[l], b_ada[l][None, :])[:bsz]
        ada4 = ada.reshape(bsz, N_MOD, 1, d)

        n_i, n_f = t // FFN_TM, D_FF // FFN_TF
        casts = ((ffn2_w_gu[l], (d // n_i, 2 * D_FF // n_f), lambda i, f: (i, f)),
                 (ffn2_w_down[l], (D_FF // n_f, d // n_i), lambda i, f: (f, i)),
                 (w_in[l], (d // n_i, 512), lambda i, f: (i, jnp.minimum(f, IN_COLS // 512 - 1))),
                 (w_out[l], (d // n_i, 256), lambda i, f: (i, jnp.minimum(f, d // 256 - 1))))
        x2d, w_gu2, w_down2, w_in_bf, w_out_bf = _ffn(
            x2d, ffn1_norm[l][None, :], ada4, 0,
            ffn1_w_gu[l].astype(BF16), ffn1_w_down[l].astype(BF16), seq, casts)

        q, k, v, cy = _mixer_in(x2d, mix_norm[l][None, :], ada4, w_in_bf, pos, freq, sign,
                                gmat, jnp.tile(q_norm[l], 16)[None, :],
                                jnp.tile(k_norm[l], 16)[None, :],
                                conv_w[l], conv_b[l][None, :], seq)
        attn = _attention(q, k, v,
                          lambda_q1[l][None, :], lambda_k1[l][None, :],
                          lambda_q2[l][None, :], lambda_k2[l][None, :],
                          subln[l][None, :], lam_init)
        x2d = _outproj(x2d, attn.reshape(t, ATTN_WIDTH), cy, conv_ln_g[l][None, :],
                       conv_ln_b[l][None, :], w_out_bf, ada4, seq)

        x2d, = _ffn(x2d, ffn2_norm[l][None, :], ada4, 6, w_gu2, w_down2, seq)
    return x2d.reshape(bsz, seq, d)
```

```python
import functools
import math

import jax
import jax.numpy as jnp
from jax import lax
from jax.experimental import pallas as pl
from jax.experimental.pallas import tpu as pltpu

F32 = jnp.float32
BF16 = jnp.bfloat16

D_MODEL = 2048
ATTN_WIDTH = 1024
CONV_WIDTH = 1024
HEAD_DIM = 64
V_DIM = 128
N_HEADS = ATTN_WIDTH // V_DIM
IN_COLS = 3 * ATTN_WIDTH + 2 * CONV_WIDTH
SEGMENT_COLS = 1024
LANES = 128
MXU_TILE = 256
CONV_K = 31
D_FF = 5632
ROPE_THETA = 10000.0
RMS_EPS = 1e-6
LN_EPS = 1e-5
FFN_RES = 0.5
N_MOD = 9

VMEM_LIMIT = 56 * 1024 * 1024
FFN_VMEM_LIMIT = 60 * 1024 * 1024
MASK_VALUE = -1e30
LOG2_E = math.log2(math.e)


def _silu(x):
    return x * jax.nn.sigmoid(x)


NORM_ROWS = 16


def _norm_modulate(h_ref, x_ref, nw_ref, sh_ref, sc_ref):
    nw, sh, sc1 = nw_ref[...], sh_ref[...], 1.0 + sc_ref[...]
    for r in range(x_ref.shape[0] // NORM_ROWS):
        rows = slice(r * NORM_ROWS, (r + 1) * NORM_ROWS)
        x = x_ref[rows, :]
        ms = jnp.mean(x * x, axis=-1, keepdims=True)
        y = x * lax.rsqrt(ms + RMS_EPS) * nw
        h_ref[rows, :] = (y * sc1 + sh).astype(BF16)


ADA_ROWS = 16
ADA_TN = 1024


def _ada_kernel(c_ref, w_ref, b_ref, o_ref):
    ca = _silu(c_ref[...])
    hi = ca.astype(BF16)
    lo = (ca - hi.astype(F32)).astype(BF16)
    w = w_ref[...].astype(BF16)
    o_ref[...] = (jnp.dot(hi, w, preferred_element_type=F32)
                  + jnp.dot(lo, w, preferred_element_type=F32) + b_ref[...])


def _ada(c_pad, w, b):
    n = w.shape[1]
    tn = ADA_TN
    return pl.pallas_call(
        _ada_kernel,
        grid=(n // tn,),
        in_specs=[pl.BlockSpec((ADA_ROWS, D_MODEL), lambda j: (0, 0)),
                  pl.BlockSpec((D_MODEL, tn), lambda j: (0, j)),
                  pl.BlockSpec((1, tn), lambda j: (0, j))],
        out_specs=pl.BlockSpec((ADA_ROWS, tn), lambda j: (0, j)),
        out_shape=jax.ShapeDtypeStruct((ADA_ROWS, n), F32),
        compiler_params=pltpu.CompilerParams(dimension_semantics=("arbitrary",),
                                             vmem_limit_bytes=VMEM_LIMIT),
        name="ada",
    )(c_pad, w, b)


def _mod_spec(k, tiles_per_batch):
    return pl.BlockSpec((None, None, 1, D_MODEL),
                        lambda i, *_: (i // tiles_per_batch, k, 0, 0))


FFN_TM = 1024
FFN_TF = 512
W_IN_CAST_COLS = 512
W_OUT_CAST_COLS = 256


def _ffn_kernel(x_ref, nw_ref, sh_ref, sc_ref, gate_ref, wg_ref, wu_ref, wd_ref, *rest, n_cast):
    cast_src, o_ref, cast_dst, h_ref = (rest[:n_cast], rest[n_cast], rest[n_cast + 1:2 * n_cast + 1],
                                        rest[2 * n_cast + 1])
    f = pl.program_id(1)

    @pl.when(f == 0)
    def _():
        _norm_modulate(h_ref, x_ref, nw_ref, sh_ref, sc_ref)
        o_ref[...] = x_ref[...]

    h = h_ref[...]
    g = jnp.dot(h, wg_ref[...], preferred_element_type=F32)
    u = jnp.dot(h, wu_ref[...], preferred_element_type=F32)
    a = (_silu(g) * u).astype(BF16)
    o_ref[...] += (FFN_RES * gate_ref[...]) * jnp.dot(a, wd_ref[...], preferred_element_type=F32)

    for src, dst in zip(cast_src, cast_dst):
        dst[...] = src[...].astype(BF16)


def _ffn(x2d, nw, ada4, mod_base, w_gu, w_down, seq_len, casts=()):
    t = x2d.shape[0]
    tm, tf = FFN_TM, FFN_TF
    n_f = D_FF // tf
    tpb = seq_len // tm
    row = pl.BlockSpec((tm, D_MODEL), lambda i, f: (i, 0))
    cast_specs = [pl.BlockSpec(shape, imap) for _, shape, imap in casts]
    outs = pl.pallas_call(
        functools.partial(_ffn_kernel, n_cast=len(casts)),
        grid=(t // tm, n_f),
        in_specs=[row,
                  pl.BlockSpec((1, D_MODEL), lambda i, f: (0, 0)),
                  _mod_spec(mod_base, tpb), _mod_spec(mod_base + 1, tpb), _mod_spec(mod_base + 2, tpb),
                  pl.BlockSpec((D_MODEL, tf), lambda i, f: (0, f)),
                  pl.BlockSpec((D_MODEL, tf), lambda i, f: (0, f + n_f)),
                  pl.BlockSpec((tf, D_MODEL), lambda i, f: (f, 0))] + cast_specs,
        out_specs=[row] + cast_specs,
        out_shape=[jax.ShapeDtypeStruct((t, D_MODEL), F32)]
                  + [jax.ShapeDtypeStruct(w.shape, BF16) for w, _, _ in casts],
        scratch_shapes=[pltpu.VMEM((tm, D_MODEL), BF16)],
        compiler_params=pltpu.CompilerParams(dimension_semantics=("parallel", "arbitrary"),
                                             vmem_limit_bytes=FFN_VMEM_LIMIT),
        name="ffn",
    )(x2d, nw, ada4, ada4, ada4, w_gu, w_gu, w_down, *[w for w, _, _ in casts])
    return outs


def _group_mean_sq(y, gmat):
    sq = y * y
    hi = sq.astype(BF16)
    lo = (sq - hi.astype(F32)).astype(BF16)
    cols = []
    for c in range(y.shape[1] // MXU_TILE):
        sl = slice(c * MXU_TILE, (c + 1) * MXU_TILE)
        cols.append(jnp.dot(hi[:, sl], gmat, preferred_element_type=F32)
                    + jnp.dot(lo[:, sl], gmat, preferred_element_type=F32))
    return jnp.concatenate(cols, axis=1)


def _rope(y, cos, sin_signed):
    lane = lax.broadcasted_iota(jnp.int32, (y.shape[0], LANES), 1)
    first_half = (lane % HEAD_DIM) < (HEAD_DIM // 2)
    cols = []
    for c in range(y.shape[1] // LANES):
        t = y[:, c * LANES:(c + 1) * LANES]
        rot = jnp.where(first_half, pltpu.roll(t, LANES - HEAD_DIM // 2, 1),
                        pltpu.roll(t, HEAD_DIM // 2, 1))
        cols.append(t * cos + rot * sin_signed)
    return jnp.concatenate(cols, axis=1)


INPROJ_ROWS = 256
CONV_HALO = 32
CONV_COLS = 128
CONV_ROWS = 64


def _mixer_in_kernel(x_ref, nw_ref, sh_ref, sc_ref, w_ref, pos_ref, freq_ref, sign_ref, gmat_ref,
                     qn_ref, kn_ref, cw_ref, cb_ref,
                     q_ref, k_ref, v_ref, y_ref,
                     h_ref, cos_ref, sin_ref, buf_ref, w8_ref, *, tiles_per_batch):
    tm = x_ref.shape[0]
    i = pl.program_id(0)

    @pl.when(i == 0)
    def _():
        buf_ref[...] = jnp.zeros_like(buf_ref)
        for t in range(CONV_K):
            w8_ref[t] = jnp.broadcast_to(cw_ref[t:t + 1, :], (8, CONV_WIDTH))

    tail = buf_ref[tm:tm + CONV_HALO, :]
    buf_ref[0:CONV_HALO, :] = jnp.where(i % tiles_per_batch == 0, 0.0, tail)

    _norm_modulate(h_ref, x_ref, nw_ref, sh_ref, sc_ref)
    ang = pos_ref[...].astype(F32) * freq_ref[...]
    cos_ref[...] = jnp.cos(ang)
    sin_ref[...] = jnp.sin(ang) * sign_ref[...]

    def segment(rows, j):
        return jnp.dot(h_ref[rows, :], w_ref[:, j * SEGMENT_COLS:(j + 1) * SEGMENT_COLS],
                       preferred_element_type=F32)

    def qk_epilogue(res, rows, norm_w, scale):
        ms = _group_mean_sq(res, gmat_ref[...])
        y = res * lax.rsqrt(ms + RMS_EPS) * norm_w
        return (_rope(y, cos_ref[rows, :], sin_ref[rows, :]) * scale).astype(BF16)

    def store_heads(o_ref, rows, val):
        for hd in range(N_HEADS):
            o_ref[hd, rows, :] = val[:, hd * V_DIM:(hd + 1) * V_DIM]

    row_blocks = [slice(r * INPROJ_ROWS, (r + 1) * INPROJ_ROWS) for r in range(tm // INPROJ_ROWS)]

    for r, rows in enumerate(row_blocks):
        lo = CONV_HALO + r * INPROJ_ROWS
        buf_ref[lo:lo + INPROJ_ROWS, :] = segment(rows, 3) * jax.nn.sigmoid(segment(rows, 4))

    def conv_piece(rb, c):
        cs = slice(c * CONV_COLS, (c + 1) * CONV_COLS)
        t0 = rb * CONV_ROWS
        first = CONV_HALO - (CONV_K - 1)
        out = None
        for s in range(8):
            rows_s = CONV_ROWS if s == 0 else CONV_ROWS + 8
            part = None
            for off in range(first, first + CONV_K):
                if off % 8 != s:
                    continue
                lo = t0 + off - s
                w_tile = jnp.tile(w8_ref[off - first, :, cs], (rows_s // 8, 1))
                term = buf_ref[lo:lo + rows_s, cs] * w_tile
                part = term if part is None else part + term
            shifted = part[s:s + CONV_ROWS, :]
            out = shifted if out is None else out + shifted
        y_ref[t0:t0 + CONV_ROWS, cs] = out + cb_ref[:, cs]

    pieces = [(rb, c) for c in range(CONV_WIDTH // CONV_COLS) for rb in range(tm // CONV_ROWS)]
    dots = [(rows, j) for rows in row_blocks for j in range(3)]
    per_dot = -(-len(pieces) // len(dots))
    for n, (rows, j) in enumerate(dots):
        for rb, c in pieces[n * per_dot:(n + 1) * per_dot]:
            conv_piece(rb, c)
        res = segment(rows, j)
        if j == 0:
            store_heads(q_ref, rows,
                        qk_epilogue(res, rows, qn_ref[...], HEAD_DIM ** -0.5 * LOG2_E))
        elif j == 1:
            store_heads(k_ref, rows, qk_epilogue(res, rows, kn_ref[...], 1.0))
        else:
            store_heads(v_ref, rows, res.astype(BF16))


def _mixer_in(x2d, nw, ada4, w_in, pos, freq, sign, gmat, qn, kn, conv_w, conv_b, seq_len, tm=512):
    t = x2d.shape[0]
    tpb = seq_len // tm
    const = lambda shape: pl.BlockSpec(shape, lambda i: (0, 0))
    out_row = pl.BlockSpec((tm, CONV_WIDTH), lambda i: (i, 0))
    head_major = pl.BlockSpec((None, N_HEADS, tm, V_DIM), lambda i: (i // tpb, 0, i % tpb, 0))
    qkv_shape = jax.ShapeDtypeStruct((t // seq_len, N_HEADS, seq_len, V_DIM), BF16)
    return pl.pallas_call(
        functools.partial(_mixer_in_kernel, tiles_per_batch=tpb),
        grid=(t // tm,),
        in_specs=[pl.BlockSpec((tm, D_MODEL), lambda i: (i, 0)),
                  const((1, D_MODEL)),
                  _mod_spec(3, tpb), _mod_spec(4, tpb),
                  pl.BlockSpec((D_MODEL, IN_COLS), lambda i: (0, 0), pipeline_mode=pl.Buffered(1)),
                  pl.BlockSpec((tm, 1), lambda i: (i, 0)),
                  const((1, LANES)), const((1, LANES)), const((MXU_TILE, MXU_TILE)),
                  const((1, ATTN_WIDTH)), const((1, ATTN_WIDTH)),
                  const((CONV_K, CONV_WIDTH)), const((1, CONV_WIDTH))],
        out_specs=[head_major, head_major, head_major, out_row],
        out_shape=[qkv_shape, qkv_shape, qkv_shape, jax.ShapeDtypeStruct((t, CONV_WIDTH), F32)],
        scratch_shapes=[pltpu.VMEM((tm, D_MODEL), BF16),
                        pltpu.VMEM((tm, LANES), F32),
                        pltpu.VMEM((tm, LANES), F32),
                        pltpu.VMEM((CONV_HALO + tm, CONV_WIDTH), F32),
                        pltpu.VMEM((CONV_K, 8, CONV_WIDTH), F32)],
        compiler_params=pltpu.CompilerParams(dimension_semantics=("arbitrary",),
                                             vmem_limit_bytes=VMEM_LIMIT),
        name="mixer_in",
    )(x2d, nw, ada4, ada4, w_in, pos, freq, sign, gmat, qn, kn, conv_w, conv_b)


ATTN_T = 512
ATTN_PV_ROWS = V_DIM + 16


def _attn_kernel(q_ref, k_ref, v_ref, lq1_ref, lk1_ref, lq2_ref, lk2_ref, subln_ref, o_ref,
                 vt_ref, qqt_ref, sa_ref, sb_ref, sc_ref, m_ref, acc_ref, *, lam_init):
    t = ATTN_T
    n_blocks = k_ref.shape[0] // t

    ones_row = (lax.broadcasted_iota(jnp.int32, (ATTN_PV_ROWS - V_DIM, t), 0) == 0).astype(BF16)
    for c in range(n_blocks):
        vt_ref[c, 0:V_DIM, :] = v_ref[c * t:(c + 1) * t, :].astype(F32).T.astype(BF16)
        vt_ref[c, V_DIM:ATTN_PV_ROWS, :] = ones_row

    lam = (jnp.exp(jnp.sum(lq1_ref[...] * lk1_ref[...], axis=-1, keepdims=True))
           - jnp.exp(jnp.sum(lq2_ref[...] * lk2_ref[...], axis=-1, keepdims=True))
           + lam_init)

    def build_q(i):
        qt = q_ref[i * t:(i + 1) * t, :].astype(F32).T
        feat = lax.broadcasted_iota(jnp.int32, qt.shape, 0)
        qqt_ref[i % 2] = jnp.concatenate([jnp.where(feat < HEAD_DIM, qt, 0.0),
                                          jnp.where(feat >= HEAD_DIM, qt, 0.0)],
                                         axis=1).astype(BF16)

    def scores(i, c, s_ref):
        s_ref[...] = jnp.dot(k_ref[c * t:(c + 1) * t, :], qqt_ref[i % 2],
                             preferred_element_type=F32)

    def softmax_pv(i, c, s_ref):
        par, width = i % 2, MXU_TILE
        for nb in range(2 * t // width):
            cols = slice(nb * width, (nb + 1) * width)
            if c == i:
                first_query = (nb * width) & (t - 1)
                n_keys = first_query + width
                s = s_ref[0:n_keys, cols]
                key = lax.broadcasted_iota(jnp.int32, s.shape, 0)
                qry = first_query + lax.broadcasted_iota(jnp.int32, s.shape, 1)
                s = jnp.where(key <= qry, s, MASK_VALUE)
            else:
                s = s_ref[:, cols]
            cmax = jnp.max(s, axis=0, keepdims=True)
            if c == i:
                p = jnp.exp2(s - cmax)
                m_ref[par, :, cols] = cmax
                acc_ref[par, :, cols] = jnp.dot(vt_ref[c, :, 0:n_keys], p.astype(BF16),
                                                preferred_element_type=F32)
            else:
                m_old = m_ref[par, :, cols]
                m_new = jnp.maximum(m_old, cmax)
                alpha = jnp.exp2(m_old - m_new)
                p = jnp.exp2(s - m_new)
                m_ref[par, :, cols] = m_new
                acc_ref[par, :, cols] = (alpha * acc_ref[par, :, cols]
                                         + jnp.dot(vt_ref[c], p.astype(BF16),
                                                   preferred_element_type=F32))

    def finalize(i):
        par = i % 2
        o = acc_ref[par, 0:V_DIM, :] / acc_ref[par, V_DIM:V_DIM + 1, :]
        d = o[:, :t] - lam * o[:, t:]
        ms = jnp.mean(d * d, axis=0, keepdims=True)
        y = (d * lax.rsqrt(ms + RMS_EPS)).T
        o_ref[i * t:(i + 1) * t, :] = (y * subln_ref[...] * (1.0 - lam_init)).astype(BF16)

    steps = [(i, c) for i in range(n_blocks) for c in [i] + list(range(i))]
    bufs = (sa_ref, sb_ref, sc_ref)
    ahead = len(bufs) - 1

    def issue(n):
        i2, c2 = steps[n]
        if c2 == i2:
            build_q(i2)
        scores(i2, c2, bufs[n % len(bufs)])

    for n in range(min(ahead, len(steps))):
        issue(n)
    for n, (i, c) in enumerate(steps):
        if n + ahead < len(steps):
            issue(n + ahead)
        softmax_pv(i, c, bufs[n % len(bufs)])
        if n + 1 == len(steps) or steps[n + 1][0] != i:
            finalize(i)


def _attention(q, k, v, lq1, lk1, lq2, lk2, subln, lam_init):
    b, _, s, _ = q.shape
    t = ATTN_T
    qkvspec = pl.BlockSpec((None, None, s, V_DIM), lambda bi, h: (bi, h, 0, 0))
    ospec = pl.BlockSpec((None, s, V_DIM), lambda bi, h: (bi, 0, h))
    vec = lambda n: pl.BlockSpec((1, n), lambda bi, h: (0, 0))
    return pl.pallas_call(
        functools.partial(_attn_kernel, lam_init=lam_init),
        grid=(b, N_HEADS),
        in_specs=[qkvspec, qkvspec, qkvspec, vec(HEAD_DIM), vec(HEAD_DIM), vec(HEAD_DIM),
                  vec(HEAD_DIM), vec(V_DIM)],
        out_specs=ospec,
        out_shape=jax.ShapeDtypeStruct((b, s, ATTN_WIDTH), BF16),
        scratch_shapes=[pltpu.VMEM((s // t, ATTN_PV_ROWS, t), BF16),
                        pltpu.VMEM((2, V_DIM, 2 * t), BF16),
                        pltpu.VMEM((t, 2 * t), F32),
                        pltpu.VMEM((t, 2 * t), F32),
                        pltpu.VMEM((t, 2 * t), F32),
                        pltpu.VMEM((2, 1, 2 * t), F32),
                        pltpu.VMEM((2, ATTN_PV_ROWS, 2 * t), F32)],
        compiler_params=pltpu.CompilerParams(dimension_semantics=("parallel", "parallel"),
                                             vmem_limit_bytes=VMEM_LIMIT),
        name="diffattn",
    )(q, k, v, lq1, lk1, lq2, lk2, subln)


def _outproj_kernel(x_ref, a_ref, y_ref, cg_ref, cbeta_ref, wa_ref, wc_ref, gate_ref, o_ref, c_ref):
    for r in range(y_ref.shape[0] // NORM_ROWS):
        rows = slice(r * NORM_ROWS, (r + 1) * NORM_ROWS)
        y = y_ref[rows, :]
        mu = jnp.mean(y, axis=-1, keepdims=True)
        yc = y - mu
        var = jnp.mean(yc * yc, axis=-1, keepdims=True)
        z = yc * lax.rsqrt(var + LN_EPS) * cg_ref[...] + cbeta_ref[...]
        c_ref[rows, :] = _silu(z).astype(BF16)
    out = (jnp.dot(a_ref[...], wa_ref[...], preferred_element_type=F32)
           + jnp.dot(c_ref[...], wc_ref[...], preferred_element_type=F32))
    o_ref[...] = x_ref[...] + gate_ref[...] * out


def _outproj(x2d, attn, conv_y, conv_g, conv_beta, w_out, ada4, seq_len, tm=512):
    t = x2d.shape[0]
    tpb = seq_len // tm
    row = pl.BlockSpec((tm, D_MODEL), lambda i: (i, 0))
    half = pl.BlockSpec((tm, ATTN_WIDTH), lambda i: (i, 0))
    vec = pl.BlockSpec((1, CONV_WIDTH), lambda i: (0, 0))
    return pl.pallas_call(
        _outproj_kernel,
        grid=(t // tm,),
        in_specs=[row, half, half, vec, vec,
                  pl.BlockSpec((ATTN_WIDTH, D_MODEL), lambda i: (0, 0)),
                  pl.BlockSpec((CONV_WIDTH, D_MODEL), lambda i: (1, 0)),
                  pl.BlockSpec((None, None, 1, D_MODEL), lambda i: (i // tpb, 5, 0, 0))],
        out_specs=row,
        out_shape=jax.ShapeDtypeStruct((t, D_MODEL), F32),
        scratch_shapes=[pltpu.VMEM((tm, CONV_WIDTH), BF16)],
        compiler_params=pltpu.CompilerParams(dimension_semantics=("parallel",),
                                             vmem_limit_bytes=VMEM_LIMIT),
        name="outproj",
    )(x2d, attn, conv_y, conv_g, conv_beta, w_out, w_out, ada4)


def kernel(x, c, positions, w_ada, b_ada, ffn1_norm, ffn1_w_gu, ffn1_w_down, mix_norm, w_in, q_norm, k_norm, lambda_q1, lambda_k1, lambda_q2, lambda_k2, subln, conv_w, conv_b, conv_ln_g, conv_ln_b, w_out, ffn2_norm, ffn2_w_gu, ffn2_w_down):
    bsz, seq, d = x.shape
    depth = w_ada.shape[0]
    t = bsz * seq

    inv_freq = ROPE_THETA ** (-jnp.arange(0, HEAD_DIM, 2, dtype=F32) / HEAD_DIM)
    half_dim = HEAD_DIM // 2
    freq = jnp.tile(inv_freq, LANES // half_dim)[None, :]
    sign = jnp.tile(jnp.concatenate([-jnp.ones(half_dim, F32), jnp.ones(half_dim, F32)]),
                    LANES // HEAD_DIM)[None, :]
    grp = jnp.arange(MXU_TILE) // HEAD_DIM
    gmat = jnp.where(grp[:, None] == grp[None, :], 1.0 / HEAD_DIM, 0.0).astype(BF16)
    pos = positions.reshape(t, 1)
    c_pad = jnp.pad(c, ((0, ADA_ROWS - bsz), (0, 0)))

    x2d = x.reshape(t, d)
    for l in range(depth):
        lam_init = 0.8 - 0.6 * math.exp(-0.3 * l)
        ada = _ada(c_pad, w_ada[l], b_ada[l][None, :])[:bsz]
        ada4 = ada.reshape(bsz, N_MOD, 1, d)

        n_i, n_f = t // FFN_TM, D_FF // FFN_TF
        casts = ((ffn2_w_gu[l], (d // n_i, 2 * D_FF // n_f), lambda i, f: (i, f)),
                 (ffn2_w_down[l], (D_FF // n_f, d // n_i), lambda i, f: (f, i)),
                 (w_in[l], (d // n_i, W_IN_CAST_COLS),
                  lambda i, f: (i, jnp.minimum(f, IN_COLS // W_IN_CAST_COLS - 1))),
                 (w_out[l], (d // n_i, W_OUT_CAST_COLS),
                  lambda i, f: (i, jnp.minimum(f, d // W_OUT_CAST_COLS - 1))))
        x2d, w_gu2, w_down2, w_in_bf, w_out_bf = _ffn(
            x2d, ffn1_norm[l][None, :], ada4, 0,
            ffn1_w_gu[l].astype(BF16), ffn1_w_down[l].astype(BF16), seq, casts)

        q, k, v, cy = _mixer_in(x2d, mix_norm[l][None, :], ada4, w_in_bf, pos, freq, sign,
                                gmat, jnp.tile(q_norm[l], ATTN_WIDTH // HEAD_DIM)[None, :],
                                jnp.tile(k_norm[l], ATTN_WIDTH // HEAD_DIM)[None, :],
                                conv_w[l], conv_b[l][None, :], seq)
        attn = _attention(q, k, v,
                          lambda_q1[l][None, :], lambda_k1[l][None, :],
                          lambda_q2[l][None, :], lambda_k2[l][None, :],
                          subln[l][None, :], lam_init)
        x2d = _outproj(x2d, attn.reshape(t, ATTN_WIDTH), cy, conv_ln_g[l][None, :],
                       conv_ln_b[l][None, :], w_out_bf, ada4, seq)

        x2d, = _ffn(x2d, ffn2_norm[l][None, :], ada4, 6, w_gu2, w_down2, seq)
    return x2d.reshape(bsz, seq, d)
```

```python
import functools
import math

import jax
import jax.numpy as jnp
from jax import lax
from jax.experimental import pallas as pl
from jax.experimental.pallas import tpu as pltpu

F32 = jnp.float32
BF16 = jnp.bfloat16

D_MODEL = 2048
ATTN_WIDTH = 1024
CONV_WIDTH = 1024
HEAD_DIM = 64
V_DIM = 128
N_HEADS = ATTN_WIDTH // V_DIM
IN_COLS = 3 * ATTN_WIDTH + 2 * CONV_WIDTH
SEGMENT_COLS = 1024
LANES = 128
MXU_TILE = 256
CONV_K = 31
D_FF = 5632
ROPE_THETA = 10000.0
RMS_EPS = 1e-6
LN_EPS = 1e-5
FFN_RES = 0.5
N_MOD = 9

VMEM_LIMIT = 56 * 1024 * 1024
FFN_VMEM_LIMIT = 60 * 1024 * 1024
MASK_VALUE = -1e30
LOG2_E = math.log2(math.e)


def _silu(x):
    return x * jax.nn.sigmoid(x)


NORM_ROWS = 16


def _norm_modulate(h_ref, x_ref, nw_ref, sh_ref, sc_ref):
    gain, sh = nw_ref[...] * (1.0 + sc_ref[...]), sh_ref[...]
    for r in range(x_ref.shape[0] // NORM_ROWS):
        rows = slice(r * NORM_ROWS, (r + 1) * NORM_ROWS)
        x = x_ref[rows, :]
        ms = jnp.mean(x * x, axis=-1, keepdims=True)
        h_ref[rows, :] = (x * lax.rsqrt(ms + RMS_EPS) * gain + sh).astype(BF16)


ADA_ROWS = 16
ADA_TN = 1024


def _ada_kernel(c_ref, w_ref, b_ref, o_ref):
    ca = _silu(c_ref[...])
    hi = ca.astype(BF16)
    lo = (ca - hi.astype(F32)).astype(BF16)
    w = w_ref[...].astype(BF16)
    o_ref[...] = (jnp.dot(hi, w, preferred_element_type=F32)
                  + jnp.dot(lo, w, preferred_element_type=F32) + b_ref[...])


def _ada(c_pad, w, b):
    n = w.shape[1]
    tn = ADA_TN
    return pl.pallas_call(
        _ada_kernel,
        grid=(n // tn,),
        in_specs=[pl.BlockSpec((ADA_ROWS, D_MODEL), lambda j: (0, 0)),
                  pl.BlockSpec((D_MODEL, tn), lambda j: (0, j)),
                  pl.BlockSpec((1, tn), lambda j: (0, j))],
        out_specs=pl.BlockSpec((ADA_ROWS, tn), lambda j: (0, j)),
        out_shape=jax.ShapeDtypeStruct((ADA_ROWS, n), F32),
        compiler_params=pltpu.CompilerParams(dimension_semantics=("arbitrary",),
                                             vmem_limit_bytes=VMEM_LIMIT),
        name="ada",
    )(c_pad, w, b)


def _mod_spec(k, tiles_per_batch):
    return pl.BlockSpec((None, None, 1, D_MODEL),
                        lambda i, *_: (i // tiles_per_batch, k, 0, 0))


FFN_TM = 1024
FFN_TF = 512
W_IN_CAST_COLS = 512
W_OUT_CAST_COLS = 256


def _ffn_kernel(x_ref, nw_ref, sh_ref, sc_ref, gate_ref, wg_ref, wu_ref, wd_ref, *rest, n_cast):
    cast_src, o_ref, cast_dst, h_ref = (rest[:n_cast], rest[n_cast], rest[n_cast + 1:2 * n_cast + 1],
                                        rest[2 * n_cast + 1])
    f = pl.program_id(1)

    @pl.when(f == 0)
    def _():
        _norm_modulate(h_ref, x_ref, nw_ref, sh_ref, sc_ref)
        o_ref[...] = x_ref[...]

    h = h_ref[...]
    g = jnp.dot(h, wg_ref[...], preferred_element_type=F32)
    u = jnp.dot(h, wu_ref[...], preferred_element_type=F32)
    a = (_silu(g) * u).astype(BF16)
    o_ref[...] += (FFN_RES * gate_ref[...]) * jnp.dot(a, wd_ref[...], preferred_element_type=F32)

    for src, dst in zip(cast_src, cast_dst):
        dst[...] = src[...].astype(BF16)


def _ffn(x2d, nw, ada4, mod_base, w_gu, w_down, seq_len, casts=()):
    t = x2d.shape[0]
    tm, tf = FFN_TM, FFN_TF
    n_f = D_FF // tf
    tpb = seq_len // tm
    row = pl.BlockSpec((tm, D_MODEL), lambda i, f: (i, 0))
    cast_specs = [pl.BlockSpec(shape, imap) for _, shape, imap in casts]
    outs = pl.pallas_call(
        functools.partial(_ffn_kernel, n_cast=len(casts)),
        grid=(t // tm, n_f),
        in_specs=[row,
                  pl.BlockSpec((1, D_MODEL), lambda i, f: (0, 0)),
                  _mod_spec(mod_base, tpb), _mod_spec(mod_base + 1, tpb), _mod_spec(mod_base + 2, tpb),
                  pl.BlockSpec((D_MODEL, tf), lambda i, f: (0, f)),
                  pl.BlockSpec((D_MODEL, tf), lambda i, f: (0, f + n_f)),
                  pl.BlockSpec((tf, D_MODEL), lambda i, f: (f, 0))] + cast_specs,
        out_specs=[row] + cast_specs,
        out_shape=[jax.ShapeDtypeStruct((t, D_MODEL), F32)]
                  + [jax.ShapeDtypeStruct(w.shape, BF16) for w, _, _ in casts],
        scratch_shapes=[pltpu.VMEM((tm, D_MODEL), BF16)],
        compiler_params=pltpu.CompilerParams(dimension_semantics=("parallel", "arbitrary"),
                                             vmem_limit_bytes=FFN_VMEM_LIMIT),
        name="ffn",
    )(x2d, nw, ada4, ada4, ada4, w_gu, w_gu, w_down, *[w for w, _, _ in casts])
    return outs


def _group_mean_sq(y, gmat):
    sq = (y * y).astype(BF16)
    cols = []
    for c in range(y.shape[1] // MXU_TILE):
        sl = slice(c * MXU_TILE, (c + 1) * MXU_TILE)
        cols.append(jnp.dot(sq[:, sl], gmat, preferred_element_type=F32))
    return jnp.concatenate(cols, axis=1)


def _rope(y, cos, sin_signed):
    lane = lax.broadcasted_iota(jnp.int32, (y.shape[0], LANES), 1)
    first_half = (lane % HEAD_DIM) < (HEAD_DIM // 2)
    cols = []
    for c in range(y.shape[1] // LANES):
        t = y[:, c * LANES:(c + 1) * LANES]
        rot = jnp.where(first_half, pltpu.roll(t, LANES - HEAD_DIM // 2, 1),
                        pltpu.roll(t, HEAD_DIM // 2, 1))
        cols.append(t * cos + rot * sin_signed)
    return jnp.concatenate(cols, axis=1)


INPROJ_ROWS = 256
CONV_HALO = 32
CONV_COLS = 128
CONV_ROWS = 64


def _mixer_in_kernel(x_ref, nw_ref, sh_ref, sc_ref, w_ref, pos_ref, freq_ref, sign_ref, gmat_ref,
                     qn_ref, kn_ref, cw_ref, cb_ref,
                     q_ref, k_ref, v_ref, y_ref,
                     h_ref, cos_ref, sin_ref, buf_ref, w8_ref, *, tiles_per_batch):
    tm = x_ref.shape[0]
    i = pl.program_id(0)

    @pl.when(i == 0)
    def _():
        buf_ref[...] = jnp.zeros_like(buf_ref)
        for t in range(CONV_K):
            w8_ref[t] = jnp.broadcast_to(cw_ref[t:t + 1, :], (8, CONV_WIDTH))

    tail = buf_ref[tm:tm + CONV_HALO, :]
    buf_ref[0:CONV_HALO, :] = jnp.where(i % tiles_per_batch == 0, 0.0, tail)

    _norm_modulate(h_ref, x_ref, nw_ref, sh_ref, sc_ref)
    ang = pos_ref[...].astype(F32) * freq_ref[...]
    cos_ref[...] = jnp.cos(ang)
    sin_ref[...] = jnp.sin(ang) * sign_ref[...]

    def segment(rows, j):
        return jnp.dot(h_ref[rows, :], w_ref[:, j * SEGMENT_COLS:(j + 1) * SEGMENT_COLS],
                       preferred_element_type=F32)

    def qk_epilogue(res, rows, norm_w, scale):
        ms = _group_mean_sq(res, gmat_ref[...])
        y = res * lax.rsqrt(ms + RMS_EPS) * norm_w
        return (_rope(y, cos_ref[rows, :], sin_ref[rows, :]) * scale).astype(BF16)

    def store_heads(o_ref, rows, val):
        for hd in range(N_HEADS):
            o_ref[hd, rows, :] = val[:, hd * V_DIM:(hd + 1) * V_DIM]

    row_blocks = [slice(r * INPROJ_ROWS, (r + 1) * INPROJ_ROWS) for r in range(tm // INPROJ_ROWS)]

    for r, rows in enumerate(row_blocks):
        lo = CONV_HALO + r * INPROJ_ROWS
        buf_ref[lo:lo + INPROJ_ROWS, :] = segment(rows, 3) * jax.nn.sigmoid(segment(rows, 4))

    def conv_piece(rb, c):
        cs = slice(c * CONV_COLS, (c + 1) * CONV_COLS)
        t0 = rb * CONV_ROWS
        first = CONV_HALO - (CONV_K - 1)
        out = None
        for s in range(8):
            rows_s = CONV_ROWS if s == 0 else CONV_ROWS + 8
            part = None
            for off in range(first, first + CONV_K):
                if off % 8 != s:
                    continue
                lo = t0 + off - s
                w_tile = jnp.tile(w8_ref[off - first, :, cs], (rows_s // 8, 1))
                term = buf_ref[lo:lo + rows_s, cs] * w_tile
                part = term if part is None else part + term
            shifted = part[s:s + CONV_ROWS, :]
            out = shifted if out is None else out + shifted
        y_ref[t0:t0 + CONV_ROWS, cs] = out + cb_ref[:, cs]

    pieces = [(rb, c) for c in range(CONV_WIDTH // CONV_COLS) for rb in range(tm // CONV_ROWS)]
    dots = [(rows, j) for rows in row_blocks for j in range(3)]
    per_dot = -(-len(pieces) // len(dots))
    for n, (rows, j) in enumerate(dots):
        for rb, c in pieces[n * per_dot:(n + 1) * per_dot]:
            conv_piece(rb, c)
        res = segment(rows, j)
        if j == 0:
            store_heads(q_ref, rows,
                        qk_epilogue(res, rows, qn_ref[...], HEAD_DIM ** -0.5 * LOG2_E))
        elif j == 1:
            store_heads(k_ref, rows, qk_epilogue(res, rows, kn_ref[...], 1.0))
        else:
            store_heads(v_ref, rows, res.astype(BF16))


def _mixer_in(x2d, nw, ada4, w_in, pos, freq, sign, gmat, qn, kn, conv_w, conv_b, seq_len, tm=512):
    t = x2d.shape[0]
    tpb = seq_len // tm
    const = lambda shape: pl.BlockSpec(shape, lambda i: (0, 0))
    out_row = pl.BlockSpec((tm, CONV_WIDTH), lambda i: (i, 0))
    head_major = pl.BlockSpec((None, N_HEADS, tm, V_DIM), lambda i: (i // tpb, 0, i % tpb, 0))
    qkv_shape = jax.ShapeDtypeStruct((t // seq_len, N_HEADS, seq_len, V_DIM), BF16)
    return pl.pallas_call(
        functools.partial(_mixer_in_kernel, tiles_per_batch=tpb),
        grid=(t // tm,),
        in_specs=[pl.BlockSpec((tm, D_MODEL), lambda i: (i, 0)),
                  const((1, D_MODEL)),
                  _mod_spec(3, tpb), _mod_spec(4, tpb),
                  pl.BlockSpec((D_MODEL, IN_COLS), lambda i: (0, 0), pipeline_mode=pl.Buffered(1)),
                  pl.BlockSpec((tm, 1), lambda i: (i, 0)),
                  const((1, LANES)), const((1, LANES)), const((MXU_TILE, MXU_TILE)),
                  const((1, ATTN_WIDTH)), const((1, ATTN_WIDTH)),
                  const((CONV_K, CONV_WIDTH)), const((1, CONV_WIDTH))],
        out_specs=[head_major, head_major, head_major, out_row],
        out_shape=[qkv_shape, qkv_shape, qkv_shape, jax.ShapeDtypeStruct((t, CONV_WIDTH), F32)],
        scratch_shapes=[pltpu.VMEM((tm, D_MODEL), BF16),
                        pltpu.VMEM((tm, LANES), F32),
                        pltpu.VMEM((tm, LANES), F32),
                        pltpu.VMEM((CONV_HALO + tm, CONV_WIDTH), F32),
                        pltpu.VMEM((CONV_K, 8, CONV_WIDTH), F32)],
        compiler_params=pltpu.CompilerParams(dimension_semantics=("arbitrary",),
                                             vmem_limit_bytes=VMEM_LIMIT),
        name="mixer_in",
    )(x2d, nw, ada4, ada4, w_in, pos, freq, sign, gmat, qn, kn, conv_w, conv_b)


ATTN_T = 512
ATTN_PV_ROWS = V_DIM + 16


def _attn_kernel(q_ref, k_ref, v_ref, lq1_ref, lk1_ref, lq2_ref, lk2_ref, subln_ref, o_ref,
                 vt_ref, qqt_ref, sa_ref, sb_ref, sc_ref, m_ref, acc_ref, *, lam_init):
    t = ATTN_T
    n_blocks = k_ref.shape[0] // t

    ones_row = (lax.broadcasted_iota(jnp.int32, (ATTN_PV_ROWS - V_DIM, t), 0) == 0).astype(BF16)
    for c in range(n_blocks):
        vt_ref[c, 0:V_DIM, :] = v_ref[c * t:(c + 1) * t, :].astype(F32).T.astype(BF16)
        vt_ref[c, V_DIM:ATTN_PV_ROWS, :] = ones_row

    lam = (jnp.exp(jnp.sum(lq1_ref[...] * lk1_ref[...], axis=-1, keepdims=True))
           - jnp.exp(jnp.sum(lq2_ref[...] * lk2_ref[...], axis=-1, keepdims=True))
           + lam_init)

    def build_q(i):
        qt = q_ref[i * t:(i + 1) * t, :].astype(F32).T
        feat = lax.broadcasted_iota(jnp.int32, qt.shape, 0)
        qqt_ref[i % 2] = jnp.concatenate([jnp.where(feat < HEAD_DIM, qt, 0.0),
                                          jnp.where(feat >= HEAD_DIM, qt, 0.0)],
                                         axis=1).astype(BF16)

    def scores(i, c, s_ref):
        s_ref[...] = jnp.dot(k_ref[c * t:(c + 1) * t, :], qqt_ref[i % 2],
                             preferred_element_type=F32)

    def softmax_pv(i, c, s_ref):
        par, width = i % 2, MXU_TILE
        for nb in range(2 * t // width):
            cols = slice(nb * width, (nb + 1) * width)
            if c == i:
                first_query = (nb * width) & (t - 1)
                n_keys = first_query + width
                s = s_ref[0:n_keys, cols]
                key = lax.broadcasted_iota(jnp.int32, s.shape, 0)
                qry = first_query + lax.broadcasted_iota(jnp.int32, s.shape, 1)
                s = jnp.where(key <= qry, s, MASK_VALUE)
            else:
                s = s_ref[:, cols]
            cmax = jnp.max(s, axis=0, keepdims=True)
            if c == i:
                p = jnp.exp2(s - cmax)
                m_ref[par, :, cols] = cmax
                acc_ref[par, :, cols] = jnp.dot(vt_ref[c, :, 0:n_keys], p.astype(BF16),
                                                preferred_element_type=F32)
            else:
                m_old = m_ref[par, :, cols]
                m_new = jnp.maximum(m_old, cmax)
                alpha = jnp.exp2(m_old - m_new)
                p = jnp.exp2(s - m_new)
                m_ref[par, :, cols] = m_new
                acc_ref[par, :, cols] = (alpha * acc_ref[par, :, cols]
                                         + jnp.dot(vt_ref[c], p.astype(BF16),
                                                   preferred_element_type=F32))

    def finalize(i):
        par = i % 2
        o = acc_ref[par, 0:V_DIM, :] / acc_ref[par, V_DIM:V_DIM + 1, :]
        d = o[:, :t] - lam * o[:, t:]
        ms = jnp.mean(d * d, axis=0, keepdims=True)
        y = (d * lax.rsqrt(ms + RMS_EPS)).T
        o_ref[i * t:(i + 1) * t, :] = (y * subln_ref[...] * (1.0 - lam_init)).astype(BF16)

    steps = [(i, c) for i in range(n_blocks) for c in [i] + list(range(i))]
    bufs = (sa_ref, sb_ref, sc_ref)
    ahead = len(bufs) - 1

    def issue(n):
        i2, c2 = steps[n]
        if c2 == i2:
            build_q(i2)
        scores(i2, c2, bufs[n % len(bufs)])

    for n in range(min(ahead, len(steps))):
        issue(n)
    for n, (i, c) in enumerate(steps):
        if n + ahead < len(steps):
            issue(n + ahead)
        softmax_pv(i, c, bufs[n % len(bufs)])
        if n + 1 == len(steps) or steps[n + 1][0] != i:
            finalize(i)


def _attention(q, k, v, lq1, lk1, lq2, lk2, subln, lam_init):
    b, _, s, _ = q.shape
    t = ATTN_T
    qkvspec = pl.BlockSpec((None, None, s, V_DIM), lambda bi, h: (bi, h, 0, 0))
    ospec = pl.BlockSpec((None, s, V_DIM), lambda bi, h: (bi, 0, h))
    vec = lambda n: pl.BlockSpec((1, n), lambda bi, h: (0, 0))
    return pl.pallas_call(
        functools.partial(_attn_kernel, lam_init=lam_init),
        grid=(b, N_HEADS),
        in_specs=[qkvspec, qkvspec, qkvspec, vec(HEAD_DIM), vec(HEAD_DIM), vec(HEAD_DIM),
                  vec(HEAD_DIM), vec(V_DIM)],
        out_specs=ospec,
        out_shape=jax.ShapeDtypeStruct((b, s, ATTN_WIDTH), BF16),
        scratch_shapes=[pltpu.VMEM((s // t, ATTN_PV_ROWS, t), BF16),
                        pltpu.VMEM((2, V_DIM, 2 * t), BF16),
                        pltpu.VMEM((t, 2 * t), F32),
                        pltpu.VMEM((t, 2 * t), F32),
                        pltpu.VMEM((t, 2 * t), F32),
                        pltpu.VMEM((2, 1, 2 * t), F32),
                        pltpu.VMEM((2, ATTN_PV_ROWS, 2 * t), F32)],
        compiler_params=pltpu.CompilerParams(dimension_semantics=("parallel", "parallel"),
                                             vmem_limit_bytes=VMEM_LIMIT),
        name="diffattn",
    )(q, k, v, lq1, lk1, lq2, lk2, subln)


def _outproj_kernel(x_ref, a_ref, y_ref, cg_ref, cbeta_ref, wa_ref, wc_ref, gate_ref, o_ref, c_ref):
    for r in range(y_ref.shape[0] // NORM_ROWS):
        rows = slice(r * NORM_ROWS, (r + 1) * NORM_ROWS)
        y = y_ref[rows, :]
        mu = jnp.mean(y, axis=-1, keepdims=True)
        yc = y - mu
        var = jnp.mean(yc * yc, axis=-1, keepdims=True)
        z = yc * lax.rsqrt(var + LN_EPS) * cg_ref[...] + cbeta_ref[...]
        c_ref[rows, :] = _silu(z).astype(BF16)
    out = (jnp.dot(a_ref[...], wa_ref[...], preferred_element_type=F32)
           + jnp.dot(c_ref[...], wc_ref[...], preferred_element_type=F32))
    o_ref[...] = x_ref[...] + gate_ref[...] * out


def _outproj(x2d, attn, conv_y, conv_g, conv_beta, w_out, ada4, seq_len, tm=512):
    t = x2d.shape[0]
    tpb = seq_len // tm
    row = pl.BlockSpec((tm, D_MODEL), lambda i: (i, 0))
    half = pl.BlockSpec((tm, ATTN_WIDTH), lambda i: (i, 0))
    vec = pl.BlockSpec((1, CONV_WIDTH), lambda i: (0, 0))
    return pl.pallas_call(
        _outproj_kernel,
        grid=(t // tm,),
        in_specs=[row, half, half, vec, vec,
                  pl.BlockSpec((ATTN_WIDTH, D_MODEL), lambda i: (0, 0)),
                  pl.BlockSpec((CONV_WIDTH, D_MODEL), lambda i: (1, 0)),
                  pl.BlockSpec((None, None, 1, D_MODEL), lambda i: (i // tpb, 5, 0, 0))],
        out_specs=row,
        out_shape=jax.ShapeDtypeStruct((t, D_MODEL), F32),
        scratch_shapes=[pltpu.VMEM((tm, CONV_WIDTH), BF16)],
        compiler_params=pltpu.CompilerParams(dimension_semantics=("parallel",),
                                             vmem_limit_bytes=VMEM_LIMIT),
        name="outproj",
    )(x2d, attn, conv_y, conv_g, conv_beta, w_out, w_out, ada4)


def kernel(x, c, positions, w_ada, b_ada, ffn1_norm, ffn1_w_gu, ffn1_w_down, mix_norm, w_in, q_norm, k_norm, lambda_q1, lambda_k1, lambda_q2, lambda_k2, subln, conv_w, conv_b, conv_ln_g, conv_ln_b, w_out, ffn2_norm, ffn2_w_gu, ffn2_w_down):
    bsz, seq, d = x.shape
    depth = w_ada.shape[0]
    t = bsz * seq

    inv_freq = ROPE_THETA ** (-jnp.arange(0, HEAD_DIM, 2, dtype=F32) / HEAD_DIM)
    half_dim = HEAD_DIM // 2
    freq = jnp.tile(inv_freq, LANES // half_dim)[None, :]
    sign = jnp.tile(jnp.concatenate([-jnp.ones(half_dim, F32), jnp.ones(half_dim, F32)]),
                    LANES // HEAD_DIM)[None, :]
    grp = jnp.arange(MXU_TILE) // HEAD_DIM
    gmat = jnp.where(grp[:, None] == grp[None, :], 1.0 / HEAD_DIM, 0.0).astype(BF16)
    pos = positions.reshape(t, 1)
    c_pad = jnp.pad(c, ((0, ADA_ROWS - bsz), (0, 0)))

    x2d = x.reshape(t, d)
    for l in range(depth):
        lam_init = 0.8 - 0.6 * math.exp(-0.3 * l)
        ada = _ada(c_pad, w_ada[l], b_ada[l][None, :])[:bsz]
        ada4 = ada.reshape(bsz, N_MOD, 1, d)

        n_i, n_f = t // FFN_TM, D_FF // FFN_TF
        casts = ((ffn2_w_gu[l], (d // n_i, 2 * D_FF // n_f), lambda i, f: (i, f)),
                 (ffn2_w_down[l], (D_FF // n_f, d // n_i), lambda i, f: (f, i)),
                 (w_in[l], (d // n_i, W_IN_CAST_COLS),
                  lambda i, f: (i, jnp.minimum(f, IN_COLS // W_IN_CAST_COLS - 1))),
                 (w_out[l], (d // n_i, W_OUT_CAST_COLS),
                  lambda i, f: (i, jnp.minimum(f, d // W_OUT_CAST_COLS - 1))))
        x2d, w_gu2, w_down2, w_in_bf, w_out_bf = _ffn(
            x2d, ffn1_norm[l][None, :], ada4, 0,
            ffn1_w_gu[l].astype(BF16), ffn1_w_down[l].astype(BF16), seq, casts)

        q, k, v, cy = _mixer_in(x2d, mix_norm[l][None, :], ada4, w_in_bf, pos, freq, sign,
                                gmat, jnp.tile(q_norm[l], ATTN_WIDTH // HEAD_DIM)[None, :],
                                jnp.tile(k_norm[l], ATTN_WIDTH // HEAD_DIM)[None, :],
                                conv_w[l], conv_b[l][None, :], seq)
        attn = _attention(q, k, v,
                          lambda_q1[l][None, :], lambda_k1[l][None, :],
                          lambda_q2[l][None, :], lambda_k2[l][None, :],
                          subln[l][None, :], lam_init)
        x2d = _outproj(x2d, attn.reshape(t, ATTN_WIDTH), cy, conv_ln_g[l][None, :],
                       conv_ln_b[l][None, :], w_out_bf, ada4, seq)

        x2d, = _ffn(x2d, ffn2_norm[l][None, :], ada4, 6, w_gu2, w_down2, seq)
    return x2d.reshape(bsz, seq, d)
```

```python
import functools
import math

import jax
import jax.numpy as jnp
from jax import lax
from jax.experimental import pallas as pl
from jax.experimental.pallas import tpu as pltpu

F32 = jnp.float32
BF16 = jnp.bfloat16

D_MODEL = 2048
ATTN_WIDTH = 1024
CONV_WIDTH = 1024
HEAD_DIM = 64
V_DIM = 128
N_HEADS = ATTN_WIDTH // V_DIM
IN_COLS = 3 * ATTN_WIDTH + 2 * CONV_WIDTH
SEGMENT_COLS = 1024
LANES = 128
MXU_TILE = 256
CONV_K = 31
D_FF = 5632
ROPE_THETA = 10000.0
RMS_EPS = 1e-6
LN_EPS = 1e-5
FFN_RES = 0.5
N_MOD = 9

VMEM_LIMIT = 56 * 1024 * 1024
FFN_VMEM_LIMIT = 60 * 1024 * 1024
MASK_VALUE = -1e30
LOG2_E = math.log2(math.e)


def _silu(x):
    return x * jax.nn.sigmoid(x)


NORM_ROWS = 16


def _norm_modulate(h_ref, x_ref, nw_ref, sh_ref, sc_ref):
    gain, sh = nw_ref[...] * (1.0 + sc_ref[...]), sh_ref[...]
    for r in range(x_ref.shape[0] // NORM_ROWS):
        rows = slice(r * NORM_ROWS, (r + 1) * NORM_ROWS)
        x = x_ref[rows, :]
        ms = jnp.mean(x * x, axis=-1, keepdims=True)
        h_ref[rows, :] = (x * lax.rsqrt(ms + RMS_EPS) * gain + sh).astype(BF16)


ADA_ROWS = 16
ADA_TN = 1024


def _ada_kernel(c_ref, w_ref, b_ref, o_ref):
    ca = _silu(c_ref[...])
    hi = ca.astype(BF16)
    lo = (ca - hi.astype(F32)).astype(BF16)
    w = w_ref[...].astype(BF16)
    o_ref[...] = (jnp.dot(hi, w, preferred_element_type=F32)
                  + jnp.dot(lo, w, preferred_element_type=F32) + b_ref[...])


def _ada(c_pad, w, b):
    n = w.shape[1]
    tn = ADA_TN
    return pl.pallas_call(
        _ada_kernel,
        grid=(n // tn,),
        in_specs=[pl.BlockSpec((ADA_ROWS, D_MODEL), lambda j: (0, 0)),
                  pl.BlockSpec((D_MODEL, tn), lambda j: (0, j)),
                  pl.BlockSpec((1, tn), lambda j: (0, j))],
        out_specs=pl.BlockSpec((ADA_ROWS, tn), lambda j: (0, j)),
        out_shape=jax.ShapeDtypeStruct((ADA_ROWS, n), F32),
        compiler_params=pltpu.CompilerParams(dimension_semantics=("arbitrary",),
                                             vmem_limit_bytes=VMEM_LIMIT),
        name="ada",
    )(c_pad, w, b)


def _mod_spec(k, tiles_per_batch):
    return pl.BlockSpec((None, None, 1, D_MODEL),
                        lambda i, *_: (i // tiles_per_batch, k, 0, 0))


FFN_TM = 1024
FFN_TF = 512
W_IN_CAST_COLS = 512
W_OUT_CAST_COLS = 256


def _ffn_kernel(x_ref, nw_ref, sh_ref, sc_ref, gate_ref, wg_ref, wu_ref, wd_ref, *rest, n_cast):
    cast_src, o_ref, cast_dst, h_ref = (rest[:n_cast], rest[n_cast], rest[n_cast + 1:2 * n_cast + 1],
                                        rest[2 * n_cast + 1])
    f = pl.program_id(1)

    @pl.when(f == 0)
    def _():
        _norm_modulate(h_ref, x_ref, nw_ref, sh_ref, sc_ref)
        o_ref[...] = x_ref[...]

    h = h_ref[...]
    g = jnp.dot(h, wg_ref[...], preferred_element_type=F32)
    u = jnp.dot(h, wu_ref[...], preferred_element_type=F32)
    a = (_silu(g) * u).astype(BF16)
    o_ref[...] += (FFN_RES * gate_ref[...]) * jnp.dot(a, wd_ref[...], preferred_element_type=F32)

    for src, dst in zip(cast_src, cast_dst):
        dst[...] = src[...].astype(BF16)


def _ffn(x2d, nw, ada4, mod_base, w_gu, w_down, seq_len, casts=()):
    t = x2d.shape[0]
    tm, tf = FFN_TM, FFN_TF
    n_f = D_FF // tf
    tpb = seq_len // tm
    row = pl.BlockSpec((tm, D_MODEL), lambda i, f: (i, 0))
    cast_specs = [pl.BlockSpec(shape, imap) for _, shape, imap in casts]
    outs = pl.pallas_call(
        functools.partial(_ffn_kernel, n_cast=len(casts)),
        grid=(t // tm, n_f),
        in_specs=[row,
                  pl.BlockSpec((1, D_MODEL), lambda i, f: (0, 0)),
                  _mod_spec(mod_base, tpb), _mod_spec(mod_base + 1, tpb), _mod_spec(mod_base + 2, tpb),
                  pl.BlockSpec((D_MODEL, tf), lambda i, f: (0, f)),
                  pl.BlockSpec((D_MODEL, tf), lambda i, f: (0, f + n_f)),
                  pl.BlockSpec((tf, D_MODEL), lambda i, f: (f, 0))] + cast_specs,
        out_specs=[row] + cast_specs,
        out_shape=[jax.ShapeDtypeStruct((t, D_MODEL), F32)]
                  + [jax.ShapeDtypeStruct(w.shape, BF16) for w, _, _ in casts],
        scratch_shapes=[pltpu.VMEM((tm, D_MODEL), BF16)],
        compiler_params=pltpu.CompilerParams(dimension_semantics=("parallel", "arbitrary"),
                                             vmem_limit_bytes=FFN_VMEM_LIMIT),
        name="ffn",
    )(x2d, nw, ada4, ada4, ada4, w_gu, w_gu, w_down, *[w for w, _, _ in casts])
    return outs


def _group_mean_sq(y, gmat):
    sq = (y * y).astype(BF16)
    cols = []
    for c in range(y.shape[1] // MXU_TILE):
        sl = slice(c * MXU_TILE, (c + 1) * MXU_TILE)
        cols.append(jnp.dot(sq[:, sl], gmat, preferred_element_type=F32))
    return jnp.concatenate(cols, axis=1)


def _rope(y, cos, sin_signed):
    lane = lax.broadcasted_iota(jnp.int32, (y.shape[0], LANES), 1)
    first_half = (lane % HEAD_DIM) < (HEAD_DIM // 2)
    cols = []
    for c in range(y.shape[1] // LANES):
        t = y[:, c * LANES:(c + 1) * LANES]
        rot = jnp.where(first_half, pltpu.roll(t, LANES - HEAD_DIM // 2, 1),
                        pltpu.roll(t, HEAD_DIM // 2, 1))
        cols.append(t * cos + rot * sin_signed)
    return jnp.concatenate(cols, axis=1)


INPROJ_ROWS = 256
CONV_HALO = 32
CONV_COLS = 128
CONV_ROWS = 64


def _mixer_in_kernel(x_ref, nw_ref, sh_ref, sc_ref, w_ref, pos_ref, freq_ref, sign_ref, gmat_ref,
                     qn_ref, kn_ref, cw_ref, cb_ref,
                     q_ref, k_ref, v_ref, y_ref,
                     h_ref, cos_ref, sin_ref, buf_ref, w8_ref, *, tiles_per_batch):
    tm = x_ref.shape[0]
    i = pl.program_id(0)

    @pl.when(i == 0)
    def _():
        buf_ref[...] = jnp.zeros_like(buf_ref)
        for t in range(CONV_K):
            w8_ref[t] = jnp.broadcast_to(cw_ref[t:t + 1, :], (8, CONV_WIDTH))

    tail = buf_ref[tm:tm + CONV_HALO, :]
    buf_ref[0:CONV_HALO, :] = jnp.where(i % tiles_per_batch == 0, 0.0, tail)

    _norm_modulate(h_ref, x_ref, nw_ref, sh_ref, sc_ref)
    ang = pos_ref[...].astype(F32) * freq_ref[...]
    cos_ref[...] = jnp.cos(ang)
    sin_ref[...] = jnp.sin(ang) * sign_ref[...]

    def segment(rows, j):
        return jnp.dot(h_ref[rows, :], w_ref[:, j * SEGMENT_COLS:(j + 1) * SEGMENT_COLS],
                       preferred_element_type=F32)

    def qk_epilogue(res, rows, norm_w, scale):
        ms = _group_mean_sq(res, gmat_ref[...])
        y = res * lax.rsqrt(ms + RMS_EPS) * norm_w
        return (_rope(y, cos_ref[rows, :], sin_ref[rows, :]) * scale).astype(BF16)

    def store_heads(o_ref, rows, val):
        for hd in range(N_HEADS):
            o_ref[hd, rows, :] = val[:, hd * V_DIM:(hd + 1) * V_DIM]

    row_blocks = [slice(r * INPROJ_ROWS, (r + 1) * INPROJ_ROWS) for r in range(tm // INPROJ_ROWS)]

    for r, rows in enumerate(row_blocks):
        lo = CONV_HALO + r * INPROJ_ROWS
        buf_ref[lo:lo + INPROJ_ROWS, :] = segment(rows, 3) * jax.nn.sigmoid(segment(rows, 4))

    def conv_piece(rb, c):
        cs = slice(c * CONV_COLS, (c + 1) * CONV_COLS)
        t0 = rb * CONV_ROWS
        first = CONV_HALO - (CONV_K - 1)
        out = None
        for s in range(8):
            rows_s = CONV_ROWS if s == 0 else CONV_ROWS + 8
            part = None
            for off in range(first, first + CONV_K):
                if off % 8 != s:
                    continue
                lo = t0 + off - s
                w_tile = jnp.tile(w8_ref[off - first, :, cs], (rows_s // 8, 1))
                term = buf_ref[lo:lo + rows_s, cs] * w_tile
                part = term if part is None else part + term
            shifted = part[s:s + CONV_ROWS, :]
            out = shifted if out is None else out + shifted
        y_ref[t0:t0 + CONV_ROWS, cs] = out + cb_ref[:, cs]

    pieces = [(rb, c) for c in range(CONV_WIDTH // CONV_COLS) for rb in range(tm // CONV_ROWS)]
    dots = [(rows, j) for rows in row_blocks for j in range(3)]
    per_dot = -(-len(pieces) // len(dots))
    for n, (rows, j) in enumerate(dots):
        for rb, c in pieces[n * per_dot:(n + 1) * per_dot]:
            conv_piece(rb, c)
        res = segment(rows, j)
        if j == 0:
            store_heads(q_ref, rows,
                        qk_epilogue(res, rows, qn_ref[...], HEAD_DIM ** -0.5 * LOG2_E))
        elif j == 1:
            store_heads(k_ref, rows, qk_epilogue(res, rows, kn_ref[...], 1.0))
        else:
            store_heads(v_ref, rows, res.astype(BF16))


def _mixer_in(x2d, nw, ada4, w_in, pos, freq, sign, gmat, qn, kn, conv_w, conv_b, seq_len, tm=512):
    t = x2d.shape[0]
    tpb = seq_len // tm
    const = lambda shape: pl.BlockSpec(shape, lambda i: (0, 0))
    out_row = pl.BlockSpec((tm, CONV_WIDTH), lambda i: (i, 0))
    head_major = pl.BlockSpec((None, N_HEADS, tm, V_DIM), lambda i: (i // tpb, 0, i % tpb, 0))
    qkv_shape = jax.ShapeDtypeStruct((t // seq_len, N_HEADS, seq_len, V_DIM), BF16)
    return pl.pallas_call(
        functools.partial(_mixer_in_kernel, tiles_per_batch=tpb),
        grid=(t // tm,),
        in_specs=[pl.BlockSpec((tm, D_MODEL), lambda i: (i, 0)),
                  const((1, D_MODEL)),
                  _mod_spec(3, tpb), _mod_spec(4, tpb),
                  pl.BlockSpec((D_MODEL, IN_COLS), lambda i: (0, 0), pipeline_mode=pl.Buffered(1)),
                  pl.BlockSpec((tm, 1), lambda i: (i, 0)),
                  const((1, LANES)), const((1, LANES)), const((MXU_TILE, MXU_TILE)),
                  const((1, ATTN_WIDTH)), const((1, ATTN_WIDTH)),
                  const((CONV_K, CONV_WIDTH)), const((1, CONV_WIDTH))],
        out_specs=[head_major, head_major, head_major, out_row],
        out_shape=[qkv_shape, qkv_shape, qkv_shape, jax.ShapeDtypeStruct((t, CONV_WIDTH), F32)],
        scratch_shapes=[pltpu.VMEM((tm, D_MODEL), BF16),
                        pltpu.VMEM((tm, LANES), F32),
                        pltpu.VMEM((tm, LANES), F32),
                        pltpu.VMEM((CONV_HALO + tm, CONV_WIDTH), F32),
                        pltpu.VMEM((CONV_K, 8, CONV_WIDTH), F32)],
        compiler_params=pltpu.CompilerParams(dimension_semantics=("arbitrary",),
                                             vmem_limit_bytes=VMEM_LIMIT),
        name="mixer_in",
    )(x2d, nw, ada4, ada4, w_in, pos, freq, sign, gmat, qn, kn, conv_w, conv_b)


ATTN_T = 512
ATTN_PV_ROWS = V_DIM + 16


def _attn_kernel(q_ref, k_ref, v_ref, lq1_ref, lk1_ref, lq2_ref, lk2_ref, subln_ref, o_ref,
                 vt_ref, qqt_ref, sa_ref, sb_ref, sc_ref, m_ref, acc_ref, *, lam_init):
    t = ATTN_T
    n_blocks = k_ref.shape[0] // t

    ones_row = (lax.broadcasted_iota(jnp.int32, (ATTN_PV_ROWS - V_DIM, t), 0) == 0).astype(BF16)
    for c in range(n_blocks):
        vt_ref[c, 0:V_DIM, :] = v_ref[c * t:(c + 1) * t, :].astype(F32).T.astype(BF16)
        vt_ref[c, V_DIM:ATTN_PV_ROWS, :] = ones_row

    lam = (jnp.exp(jnp.sum(lq1_ref[...] * lk1_ref[...], axis=-1, keepdims=True))
           - jnp.exp(jnp.sum(lq2_ref[...] * lk2_ref[...], axis=-1, keepdims=True))
           + lam_init)

    def build_q(i):
        qt = q_ref[i * t:(i + 1) * t, :].astype(F32).T
        feat = lax.broadcasted_iota(jnp.int32, qt.shape, 0)
        qqt_ref[i % 2] = jnp.concatenate([jnp.where(feat < HEAD_DIM, qt, 0.0),
                                          jnp.where(feat >= HEAD_DIM, qt, 0.0)],
                                         axis=1).astype(BF16)

    def scores(i, c, s_ref):
        s_ref[...] = jnp.dot(k_ref[c * t:(c + 1) * t, :], qqt_ref[i % 2],
                             preferred_element_type=F32)

    def softmax_pv(i, c, s_ref):
        par, width = i % 2, MXU_TILE
        for nb in range(2 * t // width):
            cols = slice(nb * width, (nb + 1) * width)
            if c == i:
                first_query = (nb * width) & (t - 1)
                n_keys = first_query + width
                s = s_ref[0:n_keys, cols]
                key = lax.broadcasted_iota(jnp.int32, s.shape, 0)
                qry = first_query + lax.broadcasted_iota(jnp.int32, s.shape, 1)
                s = jnp.where(key <= qry, s, MASK_VALUE)
            else:
                s = s_ref[:, cols]
            cmax = jnp.max(s, axis=0, keepdims=True)
            if c == i:
                p = jnp.exp2(s - cmax)
                m_ref[par, :, cols] = cmax
                acc_ref[par, :, cols] = jnp.dot(vt_ref[c, :, 0:n_keys], p.astype(BF16),
                                                preferred_element_type=F32)
            else:
                m_old = m_ref[par, :, cols]
                m_new = jnp.maximum(m_old, cmax)
                alpha = jnp.exp2(m_old - m_new)
                p = jnp.exp2(s - m_new)
                m_ref[par, :, cols] = m_new
                acc_ref[par, :, cols] = (alpha * acc_ref[par, :, cols]
                                         + jnp.dot(vt_ref[c], p.astype(BF16),
                                                   preferred_element_type=F32))

    def finalize(i):
        par = i % 2
        o = acc_ref[par, 0:V_DIM, :] / acc_ref[par, V_DIM:V_DIM + 1, :]
        d = o[:, :t] - lam * o[:, t:]
        ms = jnp.mean(d * d, axis=0, keepdims=True)
        y = (d * lax.rsqrt(ms + RMS_EPS)).T
        o_ref[i * t:(i + 1) * t, :] = (y * subln_ref[...] * (1.0 - lam_init)).astype(BF16)

    steps = [(i, c) for i in range(n_blocks) for c in [i] + list(range(i))]
    bufs = (sa_ref, sb_ref, sc_ref)
    ahead = len(bufs) - 1

    def issue(n):
        i2, c2 = steps[n]
        if c2 == i2:
            build_q(i2)
        scores(i2, c2, bufs[n % len(bufs)])

    for n in range(min(ahead, len(steps))):
        issue(n)
    for n, (i, c) in enumerate(steps):
        if n + ahead < len(steps):
            issue(n + ahead)
        softmax_pv(i, c, bufs[n % len(bufs)])
        if n + 1 == len(steps) or steps[n + 1][0] != i:
            finalize(i)


def _attention(q, k, v, lq1, lk1, lq2, lk2, subln, lam_init):
    b, _, s, _ = q.shape
    t = ATTN_T
    qkvspec = pl.BlockSpec((None, None, s, V_DIM), lambda bi, h: (bi, h, 0, 0))
    ospec = pl.BlockSpec((None, s, V_DIM), lambda bi, h: (bi, 0, h))
    vec = lambda n: pl.BlockSpec((1, n), lambda bi, h: (0, 0))
    return pl.pallas_call(
        functools.partial(_attn_kernel, lam_init=lam_init),
        grid=(b, N_HEADS),
        in_specs=[qkvspec, qkvspec, qkvspec, vec(HEAD_DIM), vec(HEAD_DIM), vec(HEAD_DIM),
                  vec(HEAD_DIM), vec(V_DIM)],
        out_specs=ospec,
        out_shape=jax.ShapeDtypeStruct((b, s, ATTN_WIDTH), BF16),
        scratch_shapes=[pltpu.VMEM((s // t, ATTN_PV_ROWS, t), BF16),
                        pltpu.VMEM((2, V_DIM, 2 * t), BF16),
                        pltpu.VMEM((t, 2 * t), F32),
                        pltpu.VMEM((t, 2 * t), F32),
                        pltpu.VMEM((t, 2 * t), F32),
                        pltpu.VMEM((2, 1, 2 * t), F32),
                        pltpu.VMEM((2, ATTN_PV_ROWS, 2 * t), F32)],
        compiler_params=pltpu.CompilerParams(dimension_semantics=("parallel", "parallel"),
                                             vmem_limit_bytes=VMEM_LIMIT),
        name="diffattn",
    )(q, k, v, lq1, lk1, lq2, lk2, subln)


OUTPROJ_ROWS = 256


def _outproj_kernel(x_ref, a_ref, y_ref, cg_ref, cbeta_ref, wa_ref, wc_ref, gate_ref, o_ref, c_ref):
    for b in range(y_ref.shape[0] // OUTPROJ_ROWS):
        blk = slice(b * OUTPROJ_ROWS, (b + 1) * OUTPROJ_ROWS)
        for r in range(b * OUTPROJ_ROWS // NORM_ROWS, (b + 1) * OUTPROJ_ROWS // NORM_ROWS):
            rows = slice(r * NORM_ROWS, (r + 1) * NORM_ROWS)
            y = y_ref[rows, :]
            mu = jnp.mean(y, axis=-1, keepdims=True)
            yc = y - mu
            var = jnp.mean(yc * yc, axis=-1, keepdims=True)
            z = yc * lax.rsqrt(var + LN_EPS) * cg_ref[...] + cbeta_ref[...]
            c_ref[rows, :] = _silu(z).astype(BF16)
        out = (jnp.dot(a_ref[blk, :], wa_ref[...], preferred_element_type=F32)
               + jnp.dot(c_ref[blk, :], wc_ref[...], preferred_element_type=F32))
        o_ref[blk, :] = x_ref[blk, :] + gate_ref[...] * out


def _outproj(x2d, attn, conv_y, conv_g, conv_beta, w_out, ada4, seq_len, tm=512):
    t = x2d.shape[0]
    tpb = seq_len // tm
    row = pl.BlockSpec((tm, D_MODEL), lambda i: (i, 0))
    half = pl.BlockSpec((tm, ATTN_WIDTH), lambda i: (i, 0))
    vec = pl.BlockSpec((1, CONV_WIDTH), lambda i: (0, 0))
    return pl.pallas_call(
        _outproj_kernel,
        grid=(t // tm,),
        in_specs=[row, half, half, vec, vec,
                  pl.BlockSpec((ATTN_WIDTH, D_MODEL), lambda i: (0, 0)),
                  pl.BlockSpec((CONV_WIDTH, D_MODEL), lambda i: (1, 0)),
                  pl.BlockSpec((None, None, 1, D_MODEL), lambda i: (i // tpb, 5, 0, 0))],
        out_specs=row,
        out_shape=jax.ShapeDtypeStruct((t, D_MODEL), F32),
        scratch_shapes=[pltpu.VMEM((tm, CONV_WIDTH), BF16)],
        compiler_params=pltpu.CompilerParams(dimension_semantics=("parallel",),
                                             vmem_limit_bytes=VMEM_LIMIT),
        name="outproj",
    )(x2d, attn, conv_y, conv_g, conv_beta, w_out, w_out, ada4)


def kernel(x, c, positions, w_ada, b_ada, ffn1_norm, ffn1_w_gu, ffn1_w_down, mix_norm, w_in, q_norm, k_norm, lambda_q1, lambda_k1, lambda_q2, lambda_k2, subln, conv_w, conv_b, conv_ln_g, conv_ln_b, w_out, ffn2_norm, ffn2_w_gu, ffn2_w_down):
    bsz, seq, d = x.shape
    depth = w_ada.shape[0]
    t = bsz * seq

    inv_freq = ROPE_THETA ** (-jnp.arange(0, HEAD_DIM, 2, dtype=F32) / HEAD_DIM)
    half_dim = HEAD_DIM // 2
    freq = jnp.tile(inv_freq, LANES // half_dim)[None, :]
    sign = jnp.tile(jnp.concatenate([-jnp.ones(half_dim, F32), jnp.ones(half_dim, F32)]),
                    LANES // HEAD_DIM)[None, :]
    grp = jnp.arange(MXU_TILE) // HEAD_DIM
    gmat = jnp.where(grp[:, None] == grp[None, :], 1.0 / HEAD_DIM, 0.0).astype(BF16)
    pos = positions.reshape(t, 1)
    c_pad = jnp.pad(c, ((0, ADA_ROWS - bsz), (0, 0)))

    x2d = x.reshape(t, d)
    for l in range(depth):
        lam_init = 0.8 - 0.6 * math.exp(-0.3 * l)
        ada = _ada(c_pad, w_ada[l], b_ada[l][None, :])[:bsz]
        ada4 = ada.reshape(bsz, N_MOD, 1, d)

        n_i, n_f = t // FFN_TM, D_FF // FFN_TF
        casts = ((ffn2_w_gu[l], (d // n_i, 2 * D_FF // n_f), lambda i, f: (i, f)),
                 (ffn2_w_down[l], (D_FF // n_f, d // n_i), lambda i, f: (f, i)),
                 (w_in[l], (d // n_i, W_IN_CAST_COLS),
                  lambda i, f: (i, jnp.minimum(f, IN_COLS // W_IN_CAST_COLS - 1))),
                 (w_out[l], (d // n_i, W_OUT_CAST_COLS),
                  lambda i, f: (i, jnp.minimum(f, d // W_OUT_CAST_COLS - 1))))
        x2d, w_gu2, w_down2, w_in_bf, w_out_bf = _ffn(
            x2d, ffn1_norm[l][None, :], ada4, 0,
            ffn1_w_gu[l].astype(BF16), ffn1_w_down[l].astype(BF16), seq, casts)

        q, k, v, cy = _mixer_in(x2d, mix_norm[l][None, :], ada4, w_in_bf, pos, freq, sign,
                                gmat, jnp.tile(q_norm[l], ATTN_WIDTH // HEAD_DIM)[None, :],
                                jnp.tile(k_norm[l], ATTN_WIDTH // HEAD_DIM)[None, :],
                                conv_w[l], conv_b[l][None, :], seq)
        attn = _attention(q, k, v,
                          lambda_q1[l][None, :], lambda_k1[l][None, :],
                          lambda_q2[l][None, :], lambda_k2[l][None, :],
                          subln[l][None, :], lam_init)
        x2d = _outproj(x2d, attn.reshape(t, ATTN_WIDTH), cy, conv_ln_g[l][None, :],
                       conv_ln_b[l][None, :], w_out_bf, ada4, seq)

        x2d, = _ffn(x2d, ffn2_norm[l][None, :], ada4, 6, w_gu2, w_down2, seq)
    return x2d.reshape(bsz, seq, d)
```
